```python
import math
import jax, jax.numpy as jnp
from jax import lax
import numpy as np

D_MODEL = 1024
BATCH = 4
SEQ = 8192
DEPTH = 2
DEC_BATCH = 8
DEC_SEQ = 16
PAST_LEN = 4096

CHUNK = 64
QBLOCK = 128
HA = 8
DHA = 64
DVA = 2 * DHA
HM = 4
DKM = 256
DVM = 256
CONV_M = 4
D_FF = 2816
CONV_F = 3
ROPE_THETA = 10000.0
LN_EPS = 1e-5
ALPHA = (2 * DEPTH) ** 0.25
BETA = (8 * DEPTH) ** -0.25

MIX_A = HA * DVA
MIX_B = HM * DVM
W_QA = HA * 2 * DHA
W_KA = HA * 2 * DHA
W_VA = HA * DVA
W_QKM = 2 * HM * DKM
W_VM = HM * DVM
W_OM = HM * DVM
W_GIF = 2 * HM
W_GA = MIX_A
W_GB = MIX_B
IN_SIZES = (W_QA, W_KA, W_VA, W_QKM, W_VM, W_OM, W_GIF, W_GA, W_GB)
D_IN = W_QA + W_KA + W_VA + W_QKM + W_VM + W_OM + W_GIF + W_GA + W_GB

kernel_name = 'hybrid_diffattn_mlstm_convffn_stream_step'


def _split_in(z):
    out = []
    o = 0
    for s in IN_SIZES:
        out.append(z[..., o:o + s])
        o += s
    return out


def _layernorm(x, g, b):
    xf = x.astype(jnp.float32)
    mu = jnp.mean(xf, axis=-1, keepdims=True)
    var = jnp.mean(jnp.square(xf - mu), axis=-1, keepdims=True)
    return ((xf - mu) * lax.rsqrt(var + LN_EPS) * g + b).astype(x.dtype)


def _rms(x):
    xf = x.astype(jnp.float32)
    return xf * lax.rsqrt(jnp.mean(jnp.square(xf), axis=-1, keepdims=True) + LN_EPS)


def _head_ln(h):
    mu = jnp.mean(h, axis=-1, keepdims=True)
    var = jnp.mean(jnp.square(h - mu), axis=-1, keepdims=True)
    return (h - mu) * lax.rsqrt(var + LN_EPS)


def _rope(x, pos):
    half = DHA // 2
    inv = ROPE_THETA ** (-jnp.arange(half, dtype=jnp.float32) * 2.0 / DHA)
    ang = pos.astype(jnp.float32)[:, None] * inv[None, :]
    cos = jnp.cos(ang)[None, :, None, None, :]
    sin = jnp.sin(ang)[None, :, None, None, :]
    xf = x.astype(jnp.float32)
    x1, x2 = xf[..., :half], xf[..., half:]
    return jnp.concatenate([x1 * cos - x2 * sin, x2 * cos + x1 * sin], axis=-1).astype(x.dtype)


def _causal_dwconv(x, buf, w, b):
    width = w.shape[0]
    L = x.shape[1]
    xp = jnp.concatenate([buf.astype(x.dtype), x], axis=1)
    y = b
    for j in range(width):
        y = y + w[j] * xp[:, j:j + L]
    return y.astype(x.dtype), xp[:, xp.shape[1] - (width - 1):]


def _diff_attn_block(q, k, v, lam, mask):
    s = jnp.einsum('bqhcd,bkhcd->bhcqk', q, k).astype(jnp.float32) * (DHA ** -0.5)
    if mask is not None:
        s = jnp.where(mask, s, -jnp.inf)
    pr = jax.nn.softmax(s, axis=-1)
    a = pr[:, :, 0] - lam * pr[:, :, 1]
    return jnp.einsum('bhqk,bkhv->bqhv', a.astype(v.dtype), v)


def _diff_attn_prompt(q, k, v, lam):
    B, S = q.shape[0], q.shape[1]
    nb = S // QBLOCK
    qb = q.reshape(B, nb, QBLOCK, HA, 2, DHA).swapaxes(0, 1)
    key_pos = jnp.arange(S)

    def block(args):
        qi, bi = args
        qpos = bi * QBLOCK + jnp.arange(QBLOCK)
        vis_end = (qpos // CHUNK + 1) * CHUNK
        mask = key_pos[None, :] < vis_end[:, None]
        return _diff_attn_block(qi, k, v, lam, mask)

    o = lax.map(block, (qb, jnp.arange(nb)))
    return o.swapaxes(0, 1).reshape(B, S, HA, DVA)


def _mlstm_chunk(carry, inp):
    C, n, m = carry
    q, k, v, ig, lf = inp
    L = q.shape[1]
    bcum = jnp.cumsum(lf, axis=1)
    causal = jnp.tril(jnp.ones((L, L), dtype=bool))[None, :, :, None]
    d = bcum[:, :, None, :] - bcum[:, None, :, :] + ig[:, None, :, :]
    d = jnp.where(causal, d, -jnp.inf)
    inter = bcum + m[:, None, :]
    m_t = jnp.maximum(inter, jnp.max(d, axis=2))
    w = jnp.exp(d - m_t[:, :, None, :])
    g = jnp.exp(inter - m_t)
    s = jnp.einsum('bthd,bshd->btsh', q, k) * w
    num = jnp.einsum('btsh,bshv->bthv', s, v) + g[..., None] * jnp.einsum('bthd,bhdv->bthv', q, C)
    den = jnp.sum(s, axis=2) + g * jnp.einsum('bthd,bhd->bth', q, n)
    den = jnp.maximum(jnp.abs(den), jnp.exp(-m_t))
    h = num / den[..., None]
    m_end = m_t[:, -1]
    w_end = jnp.exp(bcum[:, -1:] - bcum + ig - m_end[:, None, :])
    g_end = jnp.exp(bcum[:, -1] + m - m_end)
    kw = k * w_end[..., None]
    C_new = g_end[..., None, None] * C + jnp.einsum('bshd,bshv->bhdv', kw, v)
    n_new = g_end[..., None] * n + jnp.sum(kw, axis=1)
    return (C_new, n_new, m_end), h


def _mlstm_prompt(q, k, v, ig, lf):
    B, S = q.shape[0], q.shape[1]
    nc = S // CHUNK

    def to_chunks(a):
        return a.reshape((B, nc, CHUNK) + a.shape[2:]).swapaxes(0, 1)

    init = (jnp.zeros((B, HM, DKM, DVM), jnp.float32),
            jnp.zeros((B, HM, DKM), jnp.float32),
            jnp.zeros((B, HM), jnp.float32))
    state, h = lax.scan(_mlstm_chunk, init, tuple(map(to_chunks, (q, k, v, ig, lf))))
    return h.swapaxes(0, 1).reshape(B, S, HM, DVM), state


def _layer(x, pos, kv_cache, conv_m_buf, m_state, ffn_buf, p, lam_init):
    f32 = jnp.float32
    B, L = x.shape[0], x.shape[1]
    qa, ka, va, qkm, vm, om, gif, ga, gb = _split_in(x @ p['w_in'])
    qa = _rope(qa.reshape(B, L, HA, 2, DHA), pos)
    ka = _rope(ka.reshape(B, L, HA, 2, DHA), pos)
    va = va.reshape(B, L, HA, DVA)
    lp = p['lam'].astype(f32)
    lam = jnp.exp(jnp.sum(lp[0] * lp[1])) - jnp.exp(jnp.sum(lp[2] * lp[3])) + lam_init
    if kv_cache is None:
        oa = _diff_attn_prompt(qa, ka, va, lam)
    else:
        ck, cv = kv_cache
        k_all = jnp.concatenate([ck.reshape(B, ck.shape[1], HA, 2, DHA).astype(ka.dtype), ka], axis=1)
        v_all = jnp.concatenate([cv.astype(va.dtype), va], axis=1)
        oa = _diff_attn_block(qa, k_all, v_all, lam, None)
    oa = (_rms(oa) * p['subln_g'] * (1.0 - lam_init)).astype(x.dtype).reshape(B, L, MIX_A)
    qk, new_conv_m = _causal_dwconv(qkm, conv_m_buf, p['conv_m_w'], p['conv_m_b'])
    qk = jax.nn.silu(qk).astype(f32)
    qm = qk[..., :HM * DKM].reshape(B, L, HM, DKM)
    km = qk[..., HM * DKM:].reshape(B, L, HM, DKM) * (DKM ** -0.5)
    vmh = vm.astype(f32).reshape(B, L, HM, DVM)
    gif = gif.astype(f32) + p['b_if'].astype(f32)
    ig = gif[..., :HM]
    lf = jax.nn.log_sigmoid(gif[..., HM:])
    if m_state is None:
        hm, (C, n, m) = _mlstm_prompt(qm, km, vmh, ig, lf)
    else:
        init = (m_state[0].astype(f32), m_state[1].astype(f32), m_state[2].astype(f32))
        (C, n, m), hm = _mlstm_chunk(init, (qm, km, vmh, ig, lf))
    hm = (_head_ln(hm).reshape(B, L, MIX_B) * p['mh_g']).astype(x.dtype) * jax.nn.sigmoid(om)
    y = jax.nn.sigmoid(ga) * oa + jax.nn.sigmoid(gb) * hm
    x = _layernorm(ALPHA * x + y @ p['w_out'], p['ln1_g'], p['ln1_b'])
    u, new_ffn = _causal_dwconv(x @ p['w_up'], ffn_buf, p['ffn_conv_w'], p['ffn_conv_b'])
    h = jax.nn.gelu(u[..., :D_FF], approximate=False) * u[..., D_FF:]
    x = _layernorm(ALPHA * x + h @ p['w_down'], p['ln2_g'], p['ln2_b'])
    new_k = ka.reshape(B, L, HA, 2 * DHA)
    return x, (new_k, va, new_conv_m, C, n, m, new_ffn)


def setup_inputs(seed: int = 0) -> dict:
    key = jax.random.key(seed)
    ks = jax.random.split(key, 26)
    f32 = jnp.float32

    def nrm(k, shape, scale=1.0):
        return scale * jax.random.normal(k, shape, f32)

    b_if = jnp.concatenate([nrm(ks[10], (DEPTH, HM), 0.1),
                            jnp.linspace(3.0, 6.0, HM, dtype=f32)[None, :] + nrm(ks[11], (DEPTH, HM), 0.01)], axis=-1)
    return {
        'x_prompt': nrm(ks[0], (BATCH, SEQ, D_MODEL)),
        'x_sample': nrm(ks[1], (DEC_BATCH, DEC_SEQ, D_MODEL)),
        'cache_k': nrm(ks[2], (DEPTH, DEC_BATCH, PAST_LEN, HA, 2 * DHA)),
        'cache_v': nrm(ks[3], (DEPTH, DEC_BATCH, PAST_LEN, HA, DVA)),
        'state_mlstm_conv': nrm(ks[4], (DEPTH, DEC_BATCH, CONV_M - 1, W_QKM)),
        'state_mlstm_C': nrm(ks[5], (DEPTH, DEC_BATCH, HM, DKM, DVM), 0.05),
        'state_mlstm_n': nrm(ks[6], (DEPTH, DEC_BATCH, HM, DKM), 0.05),
        'state_mlstm_m': nrm(ks[7], (DEPTH, DEC_BATCH, HM)),
        'state_ffn_conv': nrm(ks[8], (DEPTH, DEC_BATCH, CONV_F - 1, 2 * D_FF)),
        'w_in': nrm(ks[9], (DEPTH, D_MODEL, D_IN), D_MODEL ** -0.5),
        'b_if': b_if,
        'mlstm_conv_w': nrm(ks[12], (DEPTH, CONV_M, W_QKM), CONV_M ** -0.5),
        'mlstm_conv_b': nrm(ks[13], (DEPTH, W_QKM), 0.01),
        'diff_lambda': nrm(ks[14], (DEPTH, 4, DHA), 0.1),
        'diff_subln_g': 1.0 + nrm(ks[15], (DEPTH, DVA), 0.01),
        'mlstm_norm_g': 1.0 + nrm(ks[16], (DEPTH, MIX_B), 0.01),
        'w_out': nrm(ks[17], (DEPTH, MIX_A, D_MODEL), BETA * MIX_A ** -0.5),
        'ln1_g': 1.0 + nrm(ks[18], (DEPTH, D_MODEL), 0.01),
        'ln1_b': nrm(ks[19], (DEPTH, D_MODEL), 0.01),
        'w_up': nrm(ks[20], (DEPTH, D_MODEL, 2 * D_FF), D_MODEL ** -0.5),
        'ffn_conv_w': nrm(ks[21], (DEPTH, CONV_F, 2 * D_FF), CONV_F ** -0.5),
        'ffn_conv_b': nrm(ks[22], (DEPTH, 2 * D_FF), 0.01),
        'w_down': nrm(ks[23], (DEPTH, D_FF, D_MODEL), BETA * D_FF ** -0.5),
        'ln2_g': 1.0 + nrm(ks[24], (DEPTH, D_MODEL), 0.01),
        'ln2_b': nrm(ks[25], (DEPTH, D_MODEL), 0.01),
    }


def _stk(lst, i):
    return jnp.stack([s[i] for s in lst])


def reference(x_prompt, x_sample, cache_k, cache_v, state_mlstm_conv, state_mlstm_C, state_mlstm_n,
              state_mlstm_m, state_ffn_conv, w_in, b_if, mlstm_conv_w, mlstm_conv_b, diff_lambda,
              diff_subln_g, mlstm_norm_g, w_out, ln1_g, ln1_b, w_up, ffn_conv_w, ffn_conv_b, w_down,
              ln2_g, ln2_b):
    xp, xs = x_prompt, x_sample
    bp = xp.shape[0]
    pos_p = jnp.arange(xp.shape[1])
    pos_s = PAST_LEN + jnp.arange(xs.shape[1])
    sp, ss = [], []
    for l in range(DEPTH):
        p = {'w_in': w_in[l], 'b_if': b_if[l], 'conv_m_w': mlstm_conv_w[l], 'conv_m_b': mlstm_conv_b[l],
             'lam': diff_lambda[l], 'subln_g': diff_subln_g[l], 'mh_g': mlstm_norm_g[l], 'w_out': w_out[l],
             'ln1_g': ln1_g[l], 'ln1_b': ln1_b[l], 'w_up': w_up[l], 'ffn_conv_w': ffn_conv_w[l],
             'ffn_conv_b': ffn_conv_b[l], 'w_down': w_down[l], 'ln2_g': ln2_g[l], 'ln2_b': ln2_b[l]}
        lam_init = 0.8 - 0.6 * math.exp(-0.3 * l)
        conv_m0 = jnp.zeros((bp, CONV_M - 1, W_QKM), xp.dtype)
        ffn0 = jnp.zeros((bp, CONV_F - 1, 2 * D_FF), xp.dtype)
        xp, st_p = _layer(xp, pos_p, None, conv_m0, None, ffn0, p, lam_init)
        xs, st_s = _layer(xs, pos_s, (cache_k[l], cache_v[l]), state_mlstm_conv[l],
                          (state_mlstm_C[l], state_mlstm_n[l], state_mlstm_m[l]), state_ffn_conv[l], p, lam_init)
        sp.append(st_p)
        ss.append(st_s)
    return (xp, xs,
            _stk(sp, 0), _stk(sp, 1), _stk(sp, 2), _stk(sp, 3), _stk(sp, 4), _stk(sp, 5), _stk(sp, 6),
            _stk(ss, 0), _stk(ss, 1), _stk(ss, 2), _stk(ss, 3), _stk(ss, 4), _stk(ss, 5), _stk(ss, 6))
```

```python
import functools
import math

import jax
import jax.numpy as jnp
from jax import lax
from jax.experimental import pallas as pl
from jax.experimental.pallas import tpu as pltpu

F32 = jnp.float32
BF16 = jnp.bfloat16

D_MODEL = 1024
DEPTH = 2
PAST_LEN = 4096
HA = 8
DHA = 64
DVA = 2 * DHA
HM = 4
DKM = 256
DVM = 256
CONV_M = 4
D_FF = 2816
CONV_F = 3
ROPE_THETA = 10000.0
LN_EPS = 1e-5
ALPHA = (2 * DEPTH) ** 0.25

W_QA = HA * 2 * DHA
W_KA = HA * 2 * DHA
W_VA = HA * DVA
W_QKM = 2 * HM * DKM
W_VM = HM * DVM
W_OM = HM * DVM
W_GIF = 2 * HM
W_GA = HA * DVA
W_GB = HM * DVM
IN_SIZES = (W_QA, W_KA, W_VA, W_QKM, W_VM, W_OM, W_GIF, W_GA, W_GB)

LANES = 128
SUBLANES = 8
NEG_BIG = -1e30
VMEM_LIMIT = 52 * 1024 * 1024


def _cparams(sem):
    return pltpu.CompilerParams(dimension_semantics=sem, vmem_limit_bytes=VMEM_LIMIT)


def _resident(shape):
    nd = len(shape)
    return pl.BlockSpec(shape, lambda *_: (0,) * nd, pipeline_mode=pl.Buffered(1))


def _layernorm_rows(r, g, b):
    mu = jnp.mean(r, axis=-1, keepdims=True)
    d = r - mu
    var = jnp.mean(d * d, axis=-1, keepdims=True)
    return d * lax.rsqrt(var + LN_EPS) * g + b


def _log_sigmoid(x):
    return jnp.minimum(x, 0.0) - jnp.log1p(jnp.exp(-jnp.abs(x)))


def _qkv_kernel(x_ref, w_ref, cos_ref, sin_ref, q_ref, kf_ref, kb_ref, vf_ref, vb_ref):
    x = x_ref[0]
    cos = cos_ref[...]
    sin = sin_ref[...]
    lane = lax.broadcasted_iota(jnp.int32, cos.shape, 1)
    first = (lane % DHA) < (DHA // 2)

    def rope(z):
        rot = jnp.where(first, pltpu.roll(z, LANES - DHA // 2, 1), pltpu.roll(z, DHA // 2, 1))
        return z * cos + rot * sin

    chunk = 512
    for c0 in range(0, W_QA, chunk):
        zq = jnp.dot(x, w_ref[:, c0:c0 + chunk], preferred_element_type=F32)
        zk = jnp.dot(x, w_ref[:, W_QA + c0:W_QA + c0 + chunk], preferred_element_type=F32)
        zv = jnp.dot(x, w_ref[:, W_QA + W_KA + c0:W_QA + W_KA + c0 + chunk], preferred_element_type=F32)
        for g0 in range(0, chunk, LANES):
            sl = slice(c0 + g0, c0 + g0 + LANES)
            q = rope(zq[:, g0:g0 + LANES]) * (DHA ** -0.5)
            k = rope(zk[:, g0:g0 + LANES])
            q_ref[0, :, sl] = q.astype(BF16)
            kf_ref[0, :, sl] = k
            kb_ref[0, :, sl] = k.astype(BF16)
        vf_ref[0, :, c0:c0 + chunk] = zv
        vb_ref[0, :, c0:c0 + chunk] = zv.astype(BF16)


def _qkv_proj(x_bf, w_qkv, cos, sin, tm):
    B, S, D = x_bf.shape
    ns = S // tm
    row = lambda b, s: (b, s, 0)
    tab = lambda b, s: (s, 0)
    out = lambda dt: jax.ShapeDtypeStruct((B, S, W_QA), dt)
    return pl.pallas_call(
        _qkv_kernel,
        grid=(B, ns),
        in_specs=[pl.BlockSpec((1, tm, D), row), _resident(w_qkv.shape),
                  pl.BlockSpec((tm, LANES), tab), pl.BlockSpec((tm, LANES), tab)],
        out_specs=[pl.BlockSpec((1, tm, W_QA), row)] * 5,
        out_shape=[out(BF16), out(F32), out(BF16), out(F32), out(BF16)],
        compiler_params=_cparams(("parallel", "parallel")),
        name="qkv_rope",
    )(x_bf, w_qkv, cos, sin)


def _proj_kernel(x_ref, *refs, n_out):
    x = x_ref[0]
    chunk = 512
    for w_ref, o_ref in zip(refs[:n_out], refs[n_out:]):
        n = w_ref.shape[1]
        for c0 in range(0, n, chunk):
            c1 = min(c0 + chunk, n)
            o_ref[0, :, c0:c1] = jnp.dot(x, w_ref[:, c0:c1], preferred_element_type=F32).astype(o_ref.dtype)


def _proj(x_bf, weights, dtypes, tm):
    B, S, D = x_bf.shape
    row = lambda b, s: (b, s, 0)
    return pl.pallas_call(
        functools.partial(_proj_kernel, n_out=len(weights)),
        grid=(B, S // tm),
        in_specs=[pl.BlockSpec((1, tm, D), row)] + [_resident(w.shape) for w in weights],
        out_specs=[pl.BlockSpec((1, tm, w.shape[1]), row) for w in weights],
        out_shape=[jax.ShapeDtypeStruct((B, S, w.shape[1]), dt) for w, dt in zip(weights, dtypes)],
        compiler_params=_cparams(("parallel", "parallel")),
        name="mix_proj",
    )(x_bf, *weights)


def _lambda(lp_ref, lam_init):
    lp = lp_ref[...]
    a = jnp.sum(lp[0:1] * lp[1:2], axis=1, keepdims=True)
    b = jnp.sum(lp[2:3] * lp[3:4], axis=1, keepdims=True)
    return jnp.exp(a) - jnp.exp(b) + lam_init


def _attn_prompt_kernel(lp_ref, g_ref, qt_ref, k_ref, vt_ref, o_ref, acc1, acc2, *, tq, chunk, lam_init):
    qi = pl.program_id(2)
    qt = qt_ref[0, 0, 0]
    row = lax.broadcasted_iota(jnp.int32, qt.shape, 0)
    zero = jnp.zeros_like(qt)
    qq = jnp.concatenate([jnp.where(row < DHA, qt, zero), jnp.where(row >= DHA, qt, zero)], axis=1)

    acc1[...] = jnp.zeros_like(acc1)
    acc2[...] = jnp.zeros_like(acc2)

    def step(j, carry, mask):
        m1, l1, m2, l2 = carry
        k = k_ref[0, pl.ds(pl.multiple_of(j * tq, tq), tq), :]
        s = jnp.dot(k, qq, preferred_element_type=F32)
        s1 = s[:, :tq]
        s2 = s[:, tq:]
        if mask is not None:
            s1 = jnp.where(mask, s1, NEG_BIG)
            s2 = jnp.where(mask, s2, NEG_BIG)
        vt = vt_ref[0, 0, j]
        n1 = jnp.maximum(m1, jnp.max(s1, axis=0, keepdims=True))
        n2 = jnp.maximum(m2, jnp.max(s2, axis=0, keepdims=True))
        p1 = jnp.exp(s1 - n1)
        p2 = jnp.exp(s2 - n2)
        a1 = jnp.exp(m1 - n1)
        a2 = jnp.exp(m2 - n2)
        l1 = a1 * l1 + jnp.sum(p1, axis=0, keepdims=True)
        l2 = a2 * l2 + jnp.sum(p2, axis=0, keepdims=True)
        acc1[...] = a1 * acc1[...] + jnp.dot(vt, p1.astype(BF16), preferred_element_type=F32)
        acc2[...] = a2 * acc2[...] + jnp.dot(vt, p2.astype(BF16), preferred_element_type=F32)
        return n1, l1, n2, l2

    init = (jnp.full((1, tq), NEG_BIG, F32), jnp.zeros((1, tq), F32),
            jnp.full((1, tq), NEG_BIG, F32), jnp.zeros((1, tq), F32))
    carry = lax.fori_loop(0, qi, lambda j, c: step(j, c, None), init)
    kr = lax.broadcasted_iota(jnp.int32, (tq, tq), 0) // chunk
    qc = lax.broadcasted_iota(jnp.int32, (tq, tq), 1) // chunk
    _, l1, _, l2 = step(qi, carry, kr <= qc)

    lam = _lambda(lp_ref, lam_init)
    o = acc1[...] / l1 - lam * (acc2[...] / l2)
    ms = jnp.mean(o * o, axis=0, keepdims=True)
    o = o * lax.rsqrt(ms + LN_EPS) * g_ref[...] * (1.0 - lam_init)
    o_ref[0] = o.T


def _attn_prompt(lp, g_col, qt, k_bf, vt, tq, chunk, lam_init):
    B, H, nq, _, _ = qt.shape
    S = k_bf.shape[1]
    return pl.pallas_call(
        functools.partial(_attn_prompt_kernel, tq=tq, chunk=chunk, lam_init=lam_init),
        grid=(B, H, nq),
        in_specs=[_resident(lp.shape), _resident(g_col.shape),
                  pl.BlockSpec((1, 1, 1, DVA, tq), lambda b, h, i: (b, h, i, 0, 0)),
                  pl.BlockSpec((1, S, DVA), lambda b, h, i: (b, 0, h)),
                  pl.BlockSpec((1, 1, nq, DVA, tq), lambda b, h, i: (b, h, 0, 0, 0))],
        out_specs=pl.BlockSpec((1, tq, DVA), lambda b, h, i: (b, i, h)),
        out_shape=jax.ShapeDtypeStruct((B, S, H * DVA), F32),
        scratch_shapes=[pltpu.VMEM((DVA, tq), F32), pltpu.VMEM((DVA, tq), F32)],
        compiler_params=_cparams(("parallel", "parallel", "arbitrary")),
        name="diff_attn_prompt",
    )(lp, g_col, qt, k_bf, vt)


def _attn_sample_kernel(lp_ref, g_ref, q_ref, ck_ref, cv_ref, nk_ref, nv_ref, o_ref, *, tk, n_new, lam_init):
    q = q_ref[0]
    L = q.shape[0]
    lane = lax.broadcasted_iota(jnp.int32, q.shape, 1)
    zero = jnp.zeros_like(q)
    qq = jnp.concatenate([jnp.where(lane < DHA, q, zero), jnp.where(lane >= DHA, q, zero)], axis=0)
    nt = (((1,), (1,)), ((), ()))

    def block(kf, vf, carry, mask):
        m, l, acc = carry
        s = lax.dot_general(qq, kf.astype(BF16), nt, preferred_element_type=F32)
        if mask is not None:
            s = jnp.where(mask, s, NEG_BIG)
        n = jnp.maximum(m, jnp.max(s, axis=1, keepdims=True))
        p = jnp.exp(s - n)
        a = jnp.exp(m - n)
        l = a * l + jnp.sum(p, axis=1, keepdims=True)
        acc = a * acc + jnp.dot(p.astype(BF16), vf.astype(BF16), preferred_element_type=F32)
        return n, l, acc

    def body(j, carry):
        sl = pl.ds(pl.multiple_of(j * tk, tk), tk)
        return block(ck_ref[0, sl, :], cv_ref[0, sl, :], carry, None)

    init = (jnp.full((2 * L, 1), NEG_BIG, F32), jnp.zeros((2 * L, 1), F32), jnp.zeros((2 * L, DVA), F32))
    carry = lax.fori_loop(0, ck_ref.shape[1] // tk, body, init)
    npad = nk_ref.shape[1]
    valid = lax.broadcasted_iota(jnp.int32, (2 * L, npad), 1) < n_new
    _, l, acc = block(nk_ref[0], nv_ref[0], carry, valid)

    lam = _lambda(lp_ref, lam_init)
    o = acc / l
    o = o[:L] - lam * o[L:]
    ms = jnp.mean(o * o, axis=1, keepdims=True)
    o_ref[0] = o * lax.rsqrt(ms + LN_EPS) * g_ref[...] * (1.0 - lam_init)


def _attn_sample(lp, g_row, q_bf, ck, cv, nk, nv, n_new, lam_init):
    B, L, _ = q_bf.shape
    P = ck.shape[1]
    npad = nk.shape[1]
    head = lambda b, h: (b, 0, h)
    return pl.pallas_call(
        functools.partial(_attn_sample_kernel, tk=512, n_new=n_new, lam_init=lam_init),
        grid=(B, HA),
        in_specs=[_resident(lp.shape), _resident(g_row.shape),
                  pl.BlockSpec((1, L, DVA), head),
                  pl.BlockSpec((1, P, DVA), head), pl.BlockSpec((1, P, DVA), head),
                  pl.BlockSpec((1, npad, DVA), head), pl.BlockSpec((1, npad, DVA), head)],
        out_specs=pl.BlockSpec((1, L, DVA), head),
        out_shape=jax.ShapeDtypeStruct((B, L, HA * DVA), F32),
        compiler_params=_cparams(("parallel", "parallel")),
        name="diff_attn_sample",
    )(lp, g_row, q_bf, ck, cv, nk, nv)


def _mlstm_kernel(bif_ref, q_ref, k_ref, v_ref, om_ref, gc_ref, gr_ref, wq_ref, wk_ref, bq_ref, bk_ref,
                  cq_ref, ck_ref, c0_ref, n0_ref, m0_ref, mg_ref,
                  h_ref, c_ref, n_ref, m_ref, qpad, kpad, *, lc):
    h = pl.program_id(1)
    c = pl.program_id(2)
    npre = CONV_M - 1
    off = SUBLANES - npre

    @pl.when(c == 0)
    def _():
        c_ref[0, 0] = c0_ref[0, 0]
        n_ref[0, 0] = n0_ref[0, 0]
        m_ref[0, 0] = m0_ref[0, 0]
        qpad[0:SUBLANES, :] = jnp.zeros((SUBLANES, DKM), F32)
        kpad[0:SUBLANES, :] = jnp.zeros((SUBLANES, DKM), F32)
        qpad[off:SUBLANES, :] = cq_ref[0]
        kpad[off:SUBLANES, :] = ck_ref[0]

    qpad[SUBLANES:SUBLANES + lc, :] = q_ref[0]
    kpad[SUBLANES:SUBLANES + lc, :] = k_ref[0]

    def conv_silu(pad, w_ref, b_ref):
        y = b_ref[...]
        for j in range(CONV_M):
            y = y + w_ref[j:j + 1, :] * pad[off + j:off + j + lc, :]
        return y * jax.nn.sigmoid(y)

    q = conv_silu(qpad, wq_ref, bq_ref)
    k = conv_silu(kpad, wk_ref, bk_ref) * (DKM ** -0.5)
    qpad[0:SUBLANES, :] = qpad[lc:lc + SUBLANES, :]
    kpad[0:SUBLANES, :] = kpad[lc:lc + SUBLANES, :]

    bi = bif_ref[h]
    bf = bif_ref[HM + h]
    gc = gc_ref[0, 0]
    gr = gr_ref[0, 0]
    ig_c = gc[:, 0:1] + bi
    lf_c = _log_sigmoid(gc[:, 1:2] + bf)
    ig_r = gr[0:1, :] + bi
    lf_r = _log_sigmoid(gr[1:2, :] + bf)

    t_i = lax.broadcasted_iota(jnp.int32, (lc, lc), 0)
    s_i = lax.broadcasted_iota(jnp.int32, (lc, lc), 1)
    causal = s_i <= t_i
    zeros = jnp.zeros((lc, lc), F32)
    bcum_c = jnp.sum(jnp.where(causal, lf_r, zeros), axis=1, keepdims=True)
    bcum_r = jnp.sum(jnp.where(t_i <= s_i, lf_c, zeros), axis=0, keepdims=True)

    m_prev = m_ref[0, 0][:, 0:1]
    d = jnp.where(causal, bcum_c - bcum_r + ig_r, NEG_BIG)
    inter = bcum_c + m_prev
    m_t = jnp.maximum(inter, jnp.max(d, axis=1, keepdims=True))
    w = jnp.exp(d - m_t)
    g = jnp.exp(inter - m_t)

    qb = q.astype(BF16)
    kb = k.astype(BF16)
    vb = v_ref[0]
    cmat = c_ref[0, 0]
    nrow = n_ref[0, 0]
    s = lax.dot_general(qb, kb, (((1,), (1,)), ((), ())), preferred_element_type=F32) * w
    num = (jnp.dot(s.astype(BF16), vb, preferred_element_type=F32)
           + g * jnp.dot(qb, cmat.astype(BF16), preferred_element_type=F32))
    den = jnp.sum(s, axis=1, keepdims=True) + g * jnp.sum(q * nrow, axis=1, keepdims=True)
    den = jnp.maximum(jnp.abs(den), jnp.exp(-m_t))
    hh = num / den

    m_end = m_t[lc - 1:lc, :]
    b_last = bcum_c[lc - 1:lc, :]
    w_end = jnp.exp(b_last - bcum_c + ig_c - m_end)
    g_end = jnp.exp(b_last + m_prev - m_end)
    kw = k * w_end
    c_ref[0, 0] = g_end * cmat + lax.dot_general(kw.astype(BF16), vb, (((0,), (0,)), ((), ())),
                                                 preferred_element_type=F32)
    n_ref[0, 0] = g_end * nrow + jnp.sum(kw, axis=0, keepdims=True)
    m_ref[0, 0] = jnp.broadcast_to(m_end, (1, LANES))

    mu = jnp.mean(hh, axis=1, keepdims=True)
    hc = hh - mu
    var = jnp.mean(hc * hc, axis=1, keepdims=True)
    h_ref[0] = hc * lax.rsqrt(var + LN_EPS) * mg_ref[...] * jax.nn.sigmoid(om_ref[0])


def _mlstm(b_if, qkm, vm_bf, om, gcol, grow, conv_w, conv_b, conv_state, c0, n0, m0, mh_g, lc):
    B, S, _ = qkm.shape
    nc = S // lc
    npre = CONV_M - 1
    qcol = lambda b, h, c: (b, c, h)
    kcol = lambda b, h, c: (b, c, HM + h)
    st = lambda b, h, c: (b, h, 0, 0)
    return pl.pallas_call(
        functools.partial(_mlstm_kernel, lc=lc),
        grid=(B, HM, nc),
        in_specs=[pl.BlockSpec(memory_space=pltpu.SMEM),
                  pl.BlockSpec((1, lc, DKM), qcol), pl.BlockSpec((1, lc, DKM), kcol),
                  pl.BlockSpec((1, lc, DVM), qcol), pl.BlockSpec((1, lc, DVM), qcol),
                  pl.BlockSpec((1, 1, lc, 2), lambda b, h, c: (b, h, c, 0)),
                  pl.BlockSpec((1, 1, 2, lc), lambda b, h, c: (b, h, 0, c)),
                  pl.BlockSpec((CONV_M, DKM), lambda b, h, c: (0, h)),
                  pl.BlockSpec((CONV_M, DKM), lambda b, h, c: (0, HM + h)),
                  pl.BlockSpec((1, DKM), lambda b, h, c: (0, h)),
                  pl.BlockSpec((1, DKM), lambda b, h, c: (0, HM + h)),
                  pl.BlockSpec((1, npre, DKM), lambda b, h, c: (b, 0, h)),
                  pl.BlockSpec((1, npre, DKM), lambda b, h, c: (b, 0, HM + h)),
                  pl.BlockSpec((1, 1, DKM, DVM), st), pl.BlockSpec((1, 1, 1, DKM), st),
                  pl.BlockSpec((1, 1, 1, LANES), st),
                  pl.BlockSpec((1, DVM), lambda b, h, c: (0, h))],
        out_specs=[pl.BlockSpec((1, lc, DVM), qcol),
                   pl.BlockSpec((1, 1, DKM, DVM), st), pl.BlockSpec((1, 1, 1, DKM), st),
                   pl.BlockSpec((1, 1, 1, LANES), st)],
        out_shape=[jax.ShapeDtypeStruct((B, S, HM * DVM), F32),
                   jax.ShapeDtypeStruct((B, HM, DKM, DVM), F32),
                   jax.ShapeDtypeStruct((B, HM, 1, DKM), F32),
                   jax.ShapeDtypeStruct((B, HM, 1, LANES), F32)],
        scratch_shapes=[pltpu.VMEM((lc + SUBLANES, DKM), F32), pltpu.VMEM((lc + SUBLANES, DKM), F32)],
        compiler_params=_cparams(("parallel", "parallel", "arbitrary")),
        name="mlstm",
    )(b_if, qkm, qkm, vm_bf, om, gcol, grow, conv_w, conv_w, conv_b, conv_b, conv_state, conv_state,
      c0, n0, m0, mh_g)


def _merge_kernel(ga_ref, oa_ref, gb_ref, hm_ref, x_ref, w_ref, g_ref, b_ref, xf_ref, xb_ref):
    y = jax.nn.sigmoid(ga_ref[0]) * oa_ref[0] + jax.nn.sigmoid(gb_ref[0]) * hm_ref[0]
    z = jnp.dot(y.astype(BF16), w_ref[...], preferred_element_type=F32)
    out = _layernorm_rows(ALPHA * x_ref[0] + z, g_ref[...], b_ref[...])
    xf_ref[0] = out
    xb_ref[0] = out.astype(BF16)


def _merge_outproj(ga, oa, gb, hm, x, w_out, g, b, tm):
    B, S, D = x.shape
    row = lambda bb, s: (bb, s, 0)
    blk = pl.BlockSpec((1, tm, D), row)
    return pl.pallas_call(
        _merge_kernel,
        grid=(B, S // tm),
        in_specs=[blk, blk, blk, blk, blk, _resident(w_out.shape), _resident(g.shape), _resident(b.shape)],
        out_specs=[blk, blk],
        out_shape=[jax.ShapeDtypeStruct((B, S, D), F32), jax.ShapeDtypeStruct((B, S, D), BF16)],
        compiler_params=_cparams(("parallel", "parallel")),
        name="merge_outproj_ln",
    )(ga, oa, gb, hm, x, w_out, g, b)


def _ffn_kernel(xb_ref, xf_ref, wu_ref, cw_ref, cb_ref, buf_ref, wd_ref, g_ref, b_ref,
                of_ref, ob_ref, tail_ref, upad, hbuf, *, tm):
    npre = CONV_F - 1
    off = SUBLANES - npre

    @pl.when(pl.program_id(1) == 0)
    def _():
        upad[0:SUBLANES, :] = jnp.zeros((SUBLANES, 2 * D_FF), F32)
        upad[off:SUBLANES, :] = buf_ref[0]

    x = xb_ref[0]
    chunk = 512
    for c0 in range(0, 2 * D_FF, chunk):
        upad[SUBLANES:SUBLANES + tm, c0:c0 + chunk] = jnp.dot(
            x, wu_ref[:, c0:c0 + chunk], preferred_element_type=F32)

    def conv(c0, c1):
        y = cb_ref[:, c0:c1]
        for j in range(CONV_F):
            y = y + cw_ref[j:j + 1, c0:c1] * upad[off + j:off + j + tm, c0:c1]
        return y

    hchunk = 256
    for c0 in range(0, D_FF, hchunk):
        a = conv(c0, c0 + hchunk)
        gate = conv(D_FF + c0, D_FF + c0 + hchunk)
        act = 0.5 * a * (1.0 + lax.erf(a * (2.0 ** -0.5)))
        hbuf[:, c0:c0 + hchunk] = (act * gate).astype(BF16)

    tail = upad[tm:tm + SUBLANES, :]
    tail_ref[0] = tail
    upad[0:SUBLANES, :] = tail

    z = jnp.dot(hbuf[...], wd_ref[...], preferred_element_type=F32)
    out = _layernorm_rows(ALPHA * xf_ref[0] + z, g_ref[...], b_ref[...])
    of_ref[0] = out
    ob_ref[0] = out.astype(BF16)


def _ffn(x_bf, x_f32, w_up, conv_w, conv_b, buf, w_down, g, b, tm):
    B, S, D = x_f32.shape
    npre = CONV_F - 1
    row = lambda bb, s: (bb, s, 0)
    blk = pl.BlockSpec((1, tm, D), row)
    per_b = lambda bb, s: (bb, 0, 0)
    return pl.pallas_call(
        functools.partial(_ffn_kernel, tm=tm),
        grid=(B, S // tm),
        in_specs=[blk, blk, _resident(w_up.shape), _resident(conv_w.shape), _resident(conv_b.shape),
                  pl.BlockSpec((1, npre, 2 * D_FF), per_b), _resident(w_down.shape),
                  _resident(g.shape), _resident(b.shape)],
        out_specs=[blk, blk, pl.BlockSpec((1, SUBLANES, 2 * D_FF), per_b)],
        out_shape=[jax.ShapeDtypeStruct((B, S, D), F32), jax.ShapeDtypeStruct((B, S, D), BF16),
                   jax.ShapeDtypeStruct((B, SUBLANES, 2 * D_FF), F32)],
        scratch_shapes=[pltpu.VMEM((tm + SUBLANES, 2 * D_FF), F32), pltpu.VMEM((tm, D_FF), BF16)],
        compiler_params=_cparams(("parallel", "arbitrary")),
        name="convffn_ln",
    )(x_bf, x_f32, w_up, conv_w, conv_b, buf, w_down, g, b)


def _rope_tables(pos):
    half = DHA // 2
    inv = ROPE_THETA ** (-jnp.arange(half, dtype=F32) * 2.0 / DHA)
    ang = pos.astype(F32)[:, None] * inv[None, :]
    cos = jnp.tile(jnp.cos(ang), (1, LANES // half))
    sin = jnp.tile(jnp.sin(ang), (1, LANES // half))
    first = (jnp.arange(LANES) % DHA) < half
    return cos, jnp.where(first[None, :], -sin, sin)


def _split_weights(w_in):
    offs = [0]
    for s in IN_SIZES:
        offs.append(offs[-1] + s)
    part = lambda i: w_in[:, offs[i]:offs[i + 1]].astype(BF16)
    w_qkv = w_in[:, :offs[3]].astype(BF16)
    w_gif = jnp.pad(part(6), ((0, 0), (0, LANES - W_GIF)))
    return w_qkv, [part(3), part(4), part(5), part(7), part(8), w_gif]


def _layer(x_f32, x_bf, pos, cache, conv_m_buf, m_state, ffn_buf, p, lam_init, cfg):
    B, S, _ = x_f32.shape
    w_qkv, w_mix = _split_weights(p['w_in'])
    cos, sin = _rope_tables(pos)
    if cfg['flat']:
        xin = x_bf.reshape(1, B * S, D_MODEL)
        cos, sin = jnp.tile(cos, (B, 1)), jnp.tile(sin, (B, 1))
    else:
        xin = x_bf
    q_bf, k_f32, k_bf, v_f32, v_bf = _qkv_proj(xin, w_qkv, cos, sin, cfg['tm_proj'])
    qkm, vm_bf, om, ga, gb, gif = _proj(xin, w_mix, [F32, BF16, F32, F32, F32, F32], cfg['tm_proj'])
    unflat = lambda a: a.reshape(B, S, a.shape[-1])
    q_bf, k_f32, k_bf, v_f32, v_bf, qkm, vm_bf, om, ga, gb, gif = map(
        unflat, (q_bf, k_f32, k_bf, v_f32, v_bf, qkm, vm_bf, om, ga, gb, gif))

    lp = p['lam'].astype(F32)
    if cache is None:
        tq = cfg['tq']
        nq = S // tq
        qt = q_bf.reshape(B, nq, tq, HA, DVA).transpose(0, 3, 1, 4, 2)
        vt = v_bf.reshape(B, nq, tq, HA, DVA).transpose(0, 3, 1, 4, 2)
        oa = _attn_prompt(lp, p['subln_g'].reshape(DVA, 1), qt, k_bf, vt, tq, cfg['chunk'], lam_init)
    else:
        ck, cv = cache
        P = ck.shape[1]
        padn = ((0, 0), (0, LANES - S), (0, 0))
        oa = _attn_sample(lp, p['subln_g'].reshape(1, DVA), q_bf,
                          ck.reshape(B, P, HA * DVA), cv.reshape(B, P, HA * DVA),
                          jnp.pad(k_f32, padn), jnp.pad(v_f32, padn), S, lam_init)

    g4 = gif[..., :W_GIF].reshape(B, S, 2, HM)
    gcol = g4.transpose(0, 3, 1, 2)
    grow = g4.transpose(0, 3, 2, 1)
    if m_state is None:
        c0 = jnp.zeros((B, HM, DKM, DVM), F32)
        n0 = jnp.zeros((B, HM, 1, DKM), F32)
        m0 = jnp.zeros((B, HM, 1, LANES), F32)
    else:
        c0 = m_state[0].astype(F32)
        n0 = m_state[1].astype(F32).reshape(B, HM, 1, DKM)
        m0 = jnp.broadcast_to(m_state[2].astype(F32)[:, :, None, None], (B, HM, 1, LANES))
    hm, c_new, n_new, m_new = _mlstm(p['b_if'].astype(F32), qkm, vm_bf, om, gcol, grow,
                                     p['conv_m_w'], p['conv_m_b'].reshape(1, W_QKM), conv_m_buf,
                                     c0, n0, m0, p['mh_g'].reshape(1, HM * DVM), cfg['lc'])
    new_conv_m = jnp.concatenate([conv_m_buf, qkm], axis=1)[:, -(CONV_M - 1):]

    row = lambda a: a.reshape(1, -1)
    x1_f32, x1_bf = _merge_outproj(ga, oa, gb, hm, x_f32, p['w_out'].astype(BF16),
                                   row(p['ln1_g']), row(p['ln1_b']), cfg['tm_merge'])
    x2_f32, x2_bf, tail = _ffn(x1_bf, x1_f32, p['w_up'].astype(BF16), p['ffn_conv_w'], row(p['ffn_conv_b']),
                               ffn_buf, p['w_down'].astype(BF16), row(p['ln2_g']), row(p['ln2_b']),
                               cfg['tm_ffn'])
    new_ffn = tail[:, SUBLANES - (CONV_F - 1):]
    new_k = k_f32.reshape(B, S, HA, 2 * DHA)
    new_v = v_f32.reshape(B, S, HA, DVA)
    state = (new_k, new_v, new_conv_m, c_new, n_new.reshape(B, HM, DKM), m_new[:, :, 0, 0], new_ffn)
    return x2_f32, x2_bf, state


def _prompt_cfg(S):
    return dict(flat=False, tm_proj=min(S, 512), tq=min(S, 256), chunk=64, lc=min(S, 256),
                tm_merge=min(S, 512), tm_ffn=min(S, 512))


def _sample_cfg(B, S):
    return dict(flat=True, tm_proj=B * S, lc=S, tm_merge=S, tm_ffn=S)


def kernel(x_prompt, x_sample, cache_k, cache_v, state_mlstm_conv, state_mlstm_C, state_mlstm_n,
           state_mlstm_m, state_ffn_conv, w_in, b_if, mlstm_conv_w, mlstm_conv_b, diff_lambda,
           diff_subln_g, mlstm_norm_g, w_out, ln1_g, ln1_b, w_up, ffn_conv_w, ffn_conv_b, w_down,
           ln2_g, ln2_b):
    bp, sp, _ = x_prompt.shape
    bs, ss, _ = x_sample.shape
    xp_f, xp_b = x_prompt, x_prompt.astype(BF16)
    xs_f, xs_b = x_sample, x_sample.astype(BF16)
    pos_p = jnp.arange(sp)
    pos_s = PAST_LEN + jnp.arange(ss)
    st_p, st_s = [], []
    for l in range(DEPTH):
        p = {'w_in': w_in[l], 'b_if': b_if[l], 'conv_m_w': mlstm_conv_w[l], 'conv_m_b': mlstm_conv_b[l],
             'lam': diff_lambda[l], 'subln_g': diff_subln_g[l], 'mh_g': mlstm_norm_g[l], 'w_out': w_out[l],
             'ln1_g': ln1_g[l], 'ln1_b': ln1_b[l], 'w_up': w_up[l], 'ffn_conv_w': ffn_conv_w[l],
             'ffn_conv_b': ffn_conv_b[l], 'w_down': w_down[l], 'ln2_g': ln2_g[l], 'ln2_b': ln2_b[l]}
        lam_init = 0.8 - 0.6 * math.exp(-0.3 * l)
        conv_m0 = jnp.zeros((bp, CONV_M - 1, W_QKM), F32)
        ffn0 = jnp.zeros((bp, CONV_F - 1, 2 * D_FF), F32)
        xp_f, xp_b, sp_l = _layer(xp_f, xp_b, pos_p, None, conv_m0, None, ffn0, p, lam_init, _prompt_cfg(sp))
        xs_f, xs_b, ss_l = _layer(xs_f, xs_b, pos_s, (cache_k[l], cache_v[l]), state_mlstm_conv[l],
                                  (state_mlstm_C[l], state_mlstm_n[l], state_mlstm_m[l]), state_ffn_conv[l],
                                  p, lam_init, _sample_cfg(bs, ss))
        st_p.append(sp_l)
        st_s.append(ss_l)
    stk = lambda lst, i: jnp.stack([s[i] for s in lst])
    return ((xp_f, xs_f) + tuple(stk(st_p, i) for i in range(7)) + tuple(stk(st_s, i) for i in range(7)))
```

```python
import functools
import math

import jax
import jax.numpy as jnp
from jax import lax
from jax.experimental import pallas as pl
from jax.experimental.pallas import tpu as pltpu

F32 = jnp.float32
BF16 = jnp.bfloat16

D_MODEL = 1024
DEPTH = 2
PAST_LEN = 4096
HA = 8
DHA = 64
DVA = 2 * DHA
HM = 4
DKM = 256
DVM = 256
CONV_M = 4
D_FF = 2816
CONV_F = 3
ROPE_THETA = 10000.0
LN_EPS = 1e-5
ALPHA = (2 * DEPTH) ** 0.25

W_QA = HA * 2 * DHA
W_KA = HA * 2 * DHA
W_VA = HA * DVA
W_QKM = 2 * HM * DKM
W_VM = HM * DVM
W_OM = HM * DVM
W_GIF = 2 * HM
W_GA = HA * DVA
W_GB = HM * DVM
IN_SIZES = (W_QA, W_KA, W_VA, W_QKM, W_VM, W_OM, W_GIF, W_GA, W_GB)

LANES = 128
SUBLANES = 8
NEG_BIG = -1e30
LOG2E = math.log2(math.e)
Q_SCALE = DHA ** -0.5 * LOG2E
VMEM_LIMIT = 52 * 1024 * 1024


def _cparams(sem):
    return pltpu.CompilerParams(dimension_semantics=sem, vmem_limit_bytes=VMEM_LIMIT)


def _resident(shape):
    nd = len(shape)
    return pl.BlockSpec(shape, lambda *_: (0,) * nd, pipeline_mode=pl.Buffered(1))


def _layernorm_rows(r, g, b):
    mu = jnp.mean(r, axis=-1, keepdims=True)
    d = r - mu
    var = jnp.mean(d * d, axis=-1, keepdims=True)
    return d * lax.rsqrt(var + LN_EPS) * g + b


def _sigmoid(x):
    return 0.5 * jnp.tanh(0.5 * x) + 0.5


def _log_sigmoid(x):
    return jnp.minimum(x, 0.0) - jnp.log1p(jnp.exp(-jnp.abs(x)))


def _rope_fn(cos_ref, sin_ref):
    cos = cos_ref[...]
    sin = sin_ref[...]
    lane = lax.broadcasted_iota(jnp.int32, cos.shape, 1)
    first = (lane % DHA) < (DHA // 2)

    def rope(z):
        rot = jnp.where(first, pltpu.roll(z, LANES - DHA // 2, 1), pltpu.roll(z, DHA // 2, 1))
        return z * cos + rot * sin

    return rope


def _qkv_dots(x, w_ref, c0, chunk):
    zq = jnp.dot(x, w_ref[:, c0:c0 + chunk], preferred_element_type=F32)
    zk = jnp.dot(x, w_ref[:, W_QA + c0:W_QA + c0 + chunk], preferred_element_type=F32)
    zv = jnp.dot(x, w_ref[:, W_QA + W_KA + c0:W_QA + W_KA + c0 + chunk], preferred_element_type=F32)
    return zq, zk, zv


def _qkv_prompt_kernel(x_ref, w_ref, cos_ref, sin_ref, *refs):
    qt_ref, kf_ref, kb_ref, vf_ref, vt_ref = refs[-5:]
    x = x_ref[0].astype(BF16)
    rope = _rope_fn(cos_ref, sin_ref)
    chunk = 512
    for c0 in range(0, W_QA, chunk):
        zq, zk, zv = _qkv_dots(x, w_ref, c0, chunk)
        for g0 in range(0, chunk, LANES):
            h = (c0 + g0) // LANES
            sl = slice(c0 + g0, c0 + g0 + LANES)
            q = rope(zq[:, g0:g0 + LANES]) * Q_SCALE
            k = rope(zk[:, g0:g0 + LANES])
            v = zv[:, g0:g0 + LANES]
            qt_ref[0, h, 0] = q.T.astype(BF16)
            rows = pl.ds(h, x.shape[0], stride=HA)
            kf_ref[0, 0, rows, :] = k
            kb_ref[0, :, sl] = k.astype(BF16)
            vf_ref[0, 0, rows, :] = v
            vt_ref[0, h, 0] = v.T.astype(BF16)


def _qkv_proj_prompt(x, w_qkv, cos, sin, tm, layer, kv_all):
    B, S, D = x.shape
    ns = S // tm
    row = lambda b, s: (b, s, 0)
    tab = lambda b, s: (s, 0)
    slot = lambda b, s: (layer, b, s, 0)
    tr = lambda b, s: (b, 0, s, 0, 0)
    in_specs = [pl.BlockSpec((1, tm, D), row), _resident(w_qkv.shape),
                pl.BlockSpec((tm, LANES), tab), pl.BlockSpec((tm, LANES), tab)]
    args = [x, w_qkv, cos, sin]
    aliases = {}
    if kv_all is not None:
        in_specs += [pl.BlockSpec(memory_space=pl.ANY)] * 2
        args += list(kv_all)
        aliases = {4: 1, 5: 3}
    t_shape = jax.ShapeDtypeStruct((B, HA, ns, DVA, tm), BF16)
    all_shape = jax.ShapeDtypeStruct((DEPTH, B, S * HA, DVA), F32)
    return pl.pallas_call(
        _qkv_prompt_kernel,
        grid=(B, ns),
        in_specs=in_specs,
        out_specs=[pl.BlockSpec((1, HA, 1, DVA, tm), tr), pl.BlockSpec((1, 1, tm * HA, DVA), slot),
                   pl.BlockSpec((1, tm, W_KA), row), pl.BlockSpec((1, 1, tm * HA, DVA), slot),
                   pl.BlockSpec((1, HA, 1, DVA, tm), tr)],
        out_shape=[t_shape, all_shape, jax.ShapeDtypeStruct((B, S, W_KA), BF16), all_shape, t_shape],
        input_output_aliases=aliases,
        compiler_params=_cparams(("parallel", "parallel")),
        name="qkv_rope_prompt",
    )(*args)


def _qkv_sample_kernel(x_ref, w_ref, cos_ref, sin_ref, q_ref, kf_ref, vf_ref):
    x = x_ref[0].astype(BF16)
    rope = _rope_fn(cos_ref, sin_ref)
    chunk = 512
    for c0 in range(0, W_QA, chunk):
        zq, zk, zv = _qkv_dots(x, w_ref, c0, chunk)
        for g0 in range(0, chunk, LANES):
            sl = slice(c0 + g0, c0 + g0 + LANES)
            q_ref[0, :, sl] = (rope(zq[:, g0:g0 + LANES]) * Q_SCALE).astype(BF16)
            kf_ref[0, :, sl] = rope(zk[:, g0:g0 + LANES])
        vf_ref[0, :, c0:c0 + chunk] = zv


def _qkv_proj_sample(x, w_qkv, cos, sin):
    B, M, D = x.shape
    out = lambda dt: jax.ShapeDtypeStruct((B, M, W_QA), dt)
    return pl.pallas_call(
        _qkv_sample_kernel,
        grid=(B,),
        in_specs=[pl.BlockSpec((1, M, D), lambda b: (b, 0, 0)), _resident(w_qkv.shape),
                  _resident(cos.shape), _resident(sin.shape)],
        out_specs=[pl.BlockSpec((1, M, W_QA), lambda b: (b, 0, 0))] * 3,
        out_shape=[out(BF16), out(F32), out(F32)],
        compiler_params=_cparams(("parallel",)),
        name="qkv_rope_sample",
    )(x, w_qkv, cos, sin)


def _proj_kernel(x_ref, *refs, n_out):
    x = x_ref[0].astype(BF16)
    chunk = 512
    for w_ref, o_ref in zip(refs[:n_out], refs[n_out:]):
        n = w_ref.shape[1]
        for c0 in range(0, n, chunk):
            c1 = min(c0 + chunk, n)
            o_ref[0, :, c0:c1] = jnp.dot(x, w_ref[:, c0:c1], preferred_element_type=F32).astype(o_ref.dtype)


def _proj(x, weights, dtypes, tm):
    B, S, D = x.shape
    row = lambda b, s: (b, s, 0)
    return pl.pallas_call(
        functools.partial(_proj_kernel, n_out=len(weights)),
        grid=(B, S // tm),
        in_specs=[pl.BlockSpec((1, tm, D), row)] + [_resident(w.shape) for w in weights],
        out_specs=[pl.BlockSpec((1, tm, w.shape[1]), row) for w in weights],
        out_shape=[jax.ShapeDtypeStruct((B, S, w.shape[1]), dt) for w, dt in zip(weights, dtypes)],
        compiler_params=_cparams(("parallel", "parallel")),
        name="mix_proj",
    )(x, *weights)


def _lambda(lp_ref, lam_init):
    lp = lp_ref[...]
    a = jnp.sum(lp[0:1] * lp[1:2], axis=1, keepdims=True)
    b = jnp.sum(lp[2:3] * lp[3:4], axis=1, keepdims=True)
    return jnp.exp(a) - jnp.exp(b) + lam_init


def _attn_prompt_kernel(lp_ref, g_ref, qt_ref, k_ref, vt_ref, o_ref, acc_ref, *, tq, tk, chunk, lam_init):
    qi = pl.program_id(2)
    r = tq // tk
    qt = jnp.concatenate([qt_ref[0, 0, t] for t in range(r)], axis=1)
    row = lax.broadcasted_iota(jnp.int32, qt.shape, 0)
    zero = jnp.zeros_like(qt)
    qq = jnp.concatenate([jnp.where(row < DHA, qt, zero), jnp.where(row >= DHA, qt, zero)], axis=1)
    acc_ref[...] = jnp.zeros_like(acc_ref)

    def step(j, carry, c0, mask):
        m, l = carry
        k = k_ref[0, pl.ds(pl.multiple_of(j * tk, tk), tk), :]
        vt = vt_ref[0, 0, j]
        w = tq - c0
        if c0:
            qs = jnp.concatenate([qq[:, c0:tq], qq[:, tq + c0:]], axis=1)
            mo = jnp.concatenate([m[:, c0:tq], m[:, tq + c0:]], axis=1)
            lo = jnp.concatenate([l[:, c0:tq], l[:, tq + c0:]], axis=1)
        else:
            qs, mo, lo = qq, m, l
        s = jnp.dot(k, qs, preferred_element_type=F32)
        if mask is not None:
            mk = mask[:, c0:]
            s = jnp.where(jnp.concatenate([mk, mk], axis=1), s, NEG_BIG)
        n = jnp.maximum(mo, jnp.max(s, axis=0, keepdims=True))
        p = jnp.exp2(s - n)
        a = jnp.exp2(mo - n)
        ln = a * lo + jnp.sum(p, axis=0, keepdims=True)
        pb = p.astype(BF16)
        acc_ref[0, :, c0:] = a[:, :w] * acc_ref[0, :, c0:] + jnp.dot(vt, pb[:, :w], preferred_element_type=F32)
        acc_ref[1, :, c0:] = a[:, w:] * acc_ref[1, :, c0:] + jnp.dot(vt, pb[:, w:], preferred_element_type=F32)
        if c0:
            n = jnp.concatenate([m[:, :c0], n[:, :w], m[:, tq:tq + c0], n[:, w:]], axis=1)
            ln = jnp.concatenate([l[:, :c0], ln[:, :w], l[:, tq:tq + c0], ln[:, w:]], axis=1)
        return n, ln

    init = (jnp.full((1, 2 * tq), NEG_BIG, F32), jnp.zeros((1, 2 * tq), F32))
    nfull = qi * r
    carry = lax.fori_loop(0, nfull, lambda j, c: step(j, c, 0, None), init)
    for d in range(r):
        kr = (lax.broadcasted_iota(jnp.int32, (tk, tq), 0) + d * tk) // chunk
        qc = lax.broadcasted_iota(jnp.int32, (tk, tq), 1) // chunk
        carry = step(nfull + d, carry, d * tk, kr <= qc)
    _, l = carry

    lam = _lambda(lp_ref, lam_init)
    o = acc_ref[0] / l[:, :tq] - lam * (acc_ref[1] / l[:, tq:])
    ms = jnp.mean(o * o, axis=0, keepdims=True)
    o = o * lax.rsqrt(ms + LN_EPS) * g_ref[...] * (1.0 - lam_init)
    o_ref[0] = o.T


def _attn_prompt(lp, g_col, qt, k_bf, vt, tq, chunk, lam_init):
    B, H, nt, _, tk = vt.shape
    S = k_bf.shape[1]
    r = tq // tk
    return pl.pallas_call(
        functools.partial(_attn_prompt_kernel, tq=tq, tk=tk, chunk=chunk, lam_init=lam_init),
        grid=(B, H, S // tq),
        in_specs=[_resident(lp.shape), _resident(g_col.shape),
                  pl.BlockSpec((1, 1, r, DVA, tk), lambda b, h, i: (b, h, i, 0, 0)),
                  pl.BlockSpec((1, S, DVA), lambda b, h, i: (b, 0, h)),
                  pl.BlockSpec((1, 1, nt, DVA, tk), lambda b, h, i: (b, h, 0, 0, 0))],
        out_specs=pl.BlockSpec((1, tq, DVA), lambda b, h, i: (b, i, h)),
        out_shape=jax.ShapeDtypeStruct((B, S, H * DVA), F32),
        scratch_shapes=[pltpu.VMEM((2, DVA, tq), F32)],
        compiler_params=_cparams(("parallel", "parallel", "arbitrary")),
        name="diff_attn_prompt",
    )(lp, g_col, qt, k_bf, vt)


def _attn_sample_kernel(lp_ref, g_ref, q_ref, ck_ref, cv_ref, nk_ref, nv_ref, o_ref, m_s, l_s, acc_s,
                        *, tkv, n_new, lam_init):
    j = pl.program_id(1)
    L = q_ref.shape[1]
    nt = (((1,), (1,)), ((), ()))

    @pl.when(j == 0)
    def _():
        m_s[...] = jnp.full(m_s.shape, NEG_BIG, F32)
        l_s[...] = jnp.zeros_like(l_s)
        acc_s[...] = jnp.zeros_like(acc_s)

    def update(h, kf, vf, mask):
        q = q_ref[0, :, h * DVA:(h + 1) * DVA]
        lane = lax.broadcasted_iota(jnp.int32, q.shape, 1)
        zero = jnp.zeros_like(q)
        qq = jnp.concatenate([jnp.where(lane < DHA, q, zero), jnp.where(lane >= DHA, q, zero)], axis=0)
        s = lax.dot_general(qq, kf.astype(BF16), nt, preferred_element_type=F32)
        if mask is not None:
            s = jnp.where(mask, s, NEG_BIG)
        m = m_s[h]
        n = jnp.maximum(m, jnp.max(s, axis=1, keepdims=True))
        p = jnp.exp2(s - n)
        a = jnp.exp2(m - n)
        l_s[h] = a * l_s[h] + jnp.sum(p, axis=1, keepdims=True)
        acc_s[h] = a * acc_s[h] + jnp.dot(p.astype(BF16), vf.astype(BF16), preferred_element_type=F32)
        m_s[h] = n

    for h in range(HA):
        rows = pl.ds(h, tkv, stride=HA)
        update(h, ck_ref[0, 0, rows, :], cv_ref[0, 0, rows, :], None)

    @pl.when(j == pl.num_programs(1) - 1)
    def _():
        valid = lax.broadcasted_iota(jnp.int32, (2 * L, nk_ref.shape[1]), 1) < n_new
        lam = _lambda(lp_ref, lam_init)
        for h in range(HA):
            c = slice(h * DVA, (h + 1) * DVA)
            update(h, nk_ref[0, :, c], nv_ref[0, :, c], valid)
            o = acc_s[h] / l_s[h]
            o = o[:L] - lam * o[L:]
            ms = jnp.mean(o * o, axis=1, keepdims=True)
            o_ref[0, :, c] = o * lax.rsqrt(ms + LN_EPS) * g_ref[...] * (1.0 - lam_init)


def _attn_sample(lp, g_row, q_bf, ck_all, cv_all, nk, nv, n_new, lam_init, layer, tkv):
    B, L, _ = q_bf.shape
    rows = ck_all.shape[2]
    npad = nk.shape[1]
    per_b = lambda b, j: (b, 0, 0)
    blk = pl.BlockSpec((1, 1, tkv * HA, DVA), lambda b, j: (layer, b, j, 0))
    return pl.pallas_call(
        functools.partial(_attn_sample_kernel, tkv=tkv, n_new=n_new, lam_init=lam_init),
        grid=(B, rows // (tkv * HA)),
        in_specs=[_resident(lp.shape), _resident(g_row.shape),
                  pl.BlockSpec((1, L, HA * DVA), per_b), blk, blk,
                  pl.BlockSpec((1, npad, HA * DVA), per_b), pl.BlockSpec((1, npad, HA * DVA), per_b)],
        out_specs=pl.BlockSpec((1, L, HA * DVA), per_b),
        out_shape=jax.ShapeDtypeStruct((B, L, HA * DVA), F32),
        scratch_shapes=[pltpu.VMEM((HA, 2 * L, 1), F32), pltpu.VMEM((HA, 2 * L, 1), F32),
                        pltpu.VMEM((HA, 2 * L, DVA), F32)],
        compiler_params=_cparams(("parallel", "arbitrary")),
        name="diff_attn_sample",
    )(lp, g_row, q_bf, ck_all, cv_all, nk, nv)


def _mlstm_kernel(bif_ref, q_ref, k_ref, v_ref, om_ref, gc_ref, gr_ref, wq_ref, wk_ref, bq_ref, bk_ref,
                  cq_ref, ck_ref, c0_ref, n0_ref, m0_ref, mg_ref,
                  h_ref, c_ref, n_ref, m_ref, qpad, kpad, *, lc):
    h = pl.program_id(1)
    c = pl.program_id(2)
    npre = CONV_M - 1
    off = SUBLANES - npre

    @pl.when(c == 0)
    def _():
        c_ref[0, 0] = c0_ref[0, 0]
        n_ref[0, 0] = n0_ref[0, 0]
        m_ref[0, 0] = m0_ref[0, 0]
        qpad[0:SUBLANES, :] = jnp.zeros((SUBLANES, DKM), F32)
        kpad[0:SUBLANES, :] = jnp.zeros((SUBLANES, DKM), F32)
        qpad[off:SUBLANES, :] = cq_ref[0]
        kpad[off:SUBLANES, :] = ck_ref[0]

    qpad[SUBLANES:SUBLANES + lc, :] = q_ref[0]
    kpad[SUBLANES:SUBLANES + lc, :] = k_ref[0]

    def conv_silu(pad, w_ref, b_ref):
        xa = pad[...]
        y = b_ref[...]
        for j in range(CONV_M):
            xs = pltpu.roll(xa, lc + SUBLANES - (off + j), 0)
            y = y + w_ref[j:j + 1, :] * xs[0:lc, :]
        return y * _sigmoid(y)

    q = conv_silu(qpad, wq_ref, bq_ref)
    k = conv_silu(kpad, wk_ref, bk_ref) * (DKM ** -0.5)
    qpad[0:SUBLANES, :] = qpad[lc:lc + SUBLANES, :]
    kpad[0:SUBLANES, :] = kpad[lc:lc + SUBLANES, :]

    bi = bif_ref[h]
    bf = bif_ref[HM + h]
    gc = gc_ref[0, 0]
    gr = gr_ref[0, 0]
    ig_c = gc[:, 0:1] + bi
    lf_c = _log_sigmoid(gc[:, 1:2] + bf)
    ig_r = gr[0:1, :] + bi
    lf_r = _log_sigmoid(gr[1:2, :] + bf)

    t_i = lax.broadcasted_iota(jnp.int32, (lc, lc), 0)
    s_i = lax.broadcasted_iota(jnp.int32, (lc, lc), 1)
    causal = s_i <= t_i
    zeros = jnp.zeros((lc, lc), F32)
    bcum_c = jnp.sum(jnp.where(causal, lf_r, zeros), axis=1, keepdims=True)
    bcum_r = jnp.sum(jnp.where(t_i <= s_i, lf_c, zeros), axis=0, keepdims=True)

    m_prev = m_ref[0, 0][:, 0:1]
    d = jnp.where(causal, bcum_c - bcum_r + ig_r, NEG_BIG)
    inter = bcum_c + m_prev
    m_t = jnp.maximum(inter, jnp.max(d, axis=1, keepdims=True))
    w = jnp.exp(d - m_t)
    g = jnp.exp(inter - m_t)

    qb = q.astype(BF16)
    kb = k.astype(BF16)
    vb = v_ref[0]
    cmat = c_ref[0, 0]
    nrow = n_ref[0, 0]
    s = lax.dot_general(qb, kb, (((1,), (1,)), ((), ())), preferred_element_type=F32) * w
    num = (jnp.dot(s.astype(BF16), vb, preferred_element_type=F32)
           + g * jnp.dot(qb, cmat.astype(BF16), preferred_element_type=F32))
    den = jnp.sum(s, axis=1, keepdims=True) + g * jnp.sum(q * nrow, axis=1, keepdims=True)
    den = jnp.maximum(jnp.abs(den), jnp.exp(-m_t))
    hh = num / den

    m_end = m_t[lc - 1:lc, :]
    b_last = bcum_c[lc - 1:lc, :]
    w_end = jnp.exp(b_last - bcum_c + ig_c - m_end)
    g_end = jnp.exp(b_last + m_prev - m_end)
    kw = k * w_end
    c_ref[0, 0] = g_end * cmat + lax.dot_general(kw.astype(BF16), vb, (((0,), (0,)), ((), ())),
                                                 preferred_element_type=F32)
    n_ref[0, 0] = g_end * nrow + jnp.sum(kw, axis=0, keepdims=True)
    m_ref[0, 0] = jnp.broadcast_to(m_end, (1, LANES))

    mu = jnp.mean(hh, axis=1, keepdims=True)
    hc = hh - mu
    var = jnp.mean(hc * hc, axis=1, keepdims=True)
    h_ref[0] = hc * lax.rsqrt(var + LN_EPS) * mg_ref[...] * _sigmoid(om_ref[0])


def _mlstm(b_if, qkm, vm_bf, om, gcol, grow, conv_w, conv_b, conv_state, c0, n0, m0, mh_g, lc):
    B, S, _ = qkm.shape
    nc = S // lc
    npre = CONV_M - 1
    qcol = lambda b, h, c: (b, c, h)
    kcol = lambda b, h, c: (b, c, HM + h)
    st = lambda b, h, c: (b, h, 0, 0)
    return pl.pallas_call(
        functools.partial(_mlstm_kernel, lc=lc),
        grid=(B, HM, nc),
        in_specs=[pl.BlockSpec(memory_space=pltpu.SMEM),
                  pl.BlockSpec((1, lc, DKM), qcol), pl.BlockSpec((1, lc, DKM), kcol),
                  pl.BlockSpec((1, lc, DVM), qcol), pl.BlockSpec((1, lc, DVM), qcol),
                  pl.BlockSpec((1, 1, lc, 2), lambda b, h, c: (b, h, c, 0)),
                  pl.BlockSpec((1, 1, 2, lc), lambda b, h, c: (b, h, 0, c)),
                  pl.BlockSpec((CONV_M, DKM), lambda b, h, c: (0, h)),
                  pl.BlockSpec((CONV_M, DKM), lambda b, h, c: (0, HM + h)),
                  pl.BlockSpec((1, DKM), lambda b, h, c: (0, h)),
                  pl.BlockSpec((1, DKM), lambda b, h, c: (0, HM + h)),
                  pl.BlockSpec((1, npre, DKM), lambda b, h, c: (b, 0, h)),
                  pl.BlockSpec((1, npre, DKM), lambda b, h, c: (b, 0, HM + h)),
                  pl.BlockSpec((1, 1, DKM, DVM), st), pl.BlockSpec((1, 1, 1, DKM), st),
                  pl.BlockSpec((1, 1, 1, LANES), st),
                  pl.BlockSpec((1, DVM), lambda b, h, c: (0, h))],
        out_specs=[pl.BlockSpec((1, lc, DVM), qcol),
                   pl.BlockSpec((1, 1, DKM, DVM), st), pl.BlockSpec((1, 1, 1, DKM), st),
                   pl.BlockSpec((1, 1, 1, LANES), st)],
        out_shape=[jax.ShapeDtypeStruct((B, S, HM * DVM), F32),
                   jax.ShapeDtypeStruct((B, HM, DKM, DVM), F32),
                   jax.ShapeDtypeStruct((B, HM, 1, DKM), F32),
                   jax.ShapeDtypeStruct((B, HM, 1, LANES), F32)],
        scratch_shapes=[pltpu.VMEM((lc + SUBLANES, DKM), F32), pltpu.VMEM((lc + SUBLANES, DKM), F32)],
        compiler_params=_cparams(("parallel", "parallel", "arbitrary")),
        name="mlstm",
    )(b_if, qkm, qkm, vm_bf, om, gcol, grow, conv_w, conv_w, conv_b, conv_b, conv_state, conv_state,
      c0, n0, m0, mh_g)


def _merge_kernel(ga_ref, oa_ref, gb_ref, hm_ref, x_ref, w_ref, g_ref, b_ref, xf_ref, xb_ref):
    y = _sigmoid(ga_ref[0]) * oa_ref[0] + _sigmoid(gb_ref[0]) * hm_ref[0]
    z = jnp.dot(y.astype(BF16), w_ref[...], preferred_element_type=F32)
    out = _layernorm_rows(ALPHA * x_ref[0] + z, g_ref[...], b_ref[...])
    xf_ref[0] = out
    xb_ref[0] = out.astype(BF16)


def _merge_outproj(ga, oa, gb, hm, x, w_out, g, b, tm):
    B, S, D = x.shape
    row = lambda bb, s: (bb, s, 0)
    blk = pl.BlockSpec((1, tm, D), row)
    return pl.pallas_call(
        _merge_kernel,
        grid=(B, S // tm),
        in_specs=[blk, blk, blk, blk, blk, _resident(w_out.shape), _resident(g.shape), _resident(b.shape)],
        out_specs=[blk, blk],
        out_shape=[jax.ShapeDtypeStruct((B, S, D), F32), jax.ShapeDtypeStruct((B, S, D), BF16)],
        compiler_params=_cparams(("parallel", "parallel")),
        name="merge_outproj_ln",
    )(ga, oa, gb, hm, x, w_out, g, b)


def _ffn_kernel(xb_ref, xf_ref, wu_ref, cw_ref, cb_ref, buf_ref, wd_ref, g_ref, b_ref,
                of_ref, ob_ref, tail_ref, upad, hbuf, *, tm):
    npre = CONV_F - 1
    off = SUBLANES - npre

    @pl.when(pl.program_id(1) == 0)
    def _():
        upad[0:SUBLANES, :] = jnp.zeros((SUBLANES, 2 * D_FF), F32)
        upad[off:SUBLANES, :] = buf_ref[0]

    x = xb_ref[0]
    chunk = 512
    for c0 in range(0, 2 * D_FF, chunk):
        upad[SUBLANES:SUBLANES + tm, c0:c0 + chunk] = jnp.dot(
            x, wu_ref[:, c0:c0 + chunk], preferred_element_type=F32)

    def conv(c0, c1):
        y = cb_ref[:, c0:c1]
        for j in range(CONV_F):
            y = y + cw_ref[j:j + 1, c0:c1] * upad[off + j:off + j + tm, c0:c1]
        return y

    hchunk = 256
    for c0 in range(0, D_FF, hchunk):
        a = conv(c0, c0 + hchunk)
        gate = conv(D_FF + c0, D_FF + c0 + hchunk)
        act = 0.5 * a * (1.0 + lax.erf(a * (2.0 ** -0.5)))
        hbuf[:, c0:c0 + hchunk] = (act * gate).astype(BF16)

    tail = upad[tm:tm + SUBLANES, :]
    tail_ref[0] = tail
    upad[0:SUBLANES, :] = tail

    z = jnp.dot(hbuf[...], wd_ref[...], preferred_element_type=F32)
    out = _layernorm_rows(ALPHA * xf_ref[0] + z, g_ref[...], b_ref[...])
    of_ref[0] = out
    ob_ref[0] = out.astype(BF16)


def _ffn(x_bf, x_f32, w_up, conv_w, conv_b, buf, w_down, g, b, tm):
    B, S, D = x_f32.shape
    npre = CONV_F - 1
    row = lambda bb, s: (bb, s, 0)
    blk = pl.BlockSpec((1, tm, D), row)
    per_b = lambda bb, s: (bb, 0, 0)
    return pl.pallas_call(
        functools.partial(_ffn_kernel, tm=tm),
        grid=(B, S // tm),
        in_specs=[blk, blk, _resident(w_up.shape), _resident(conv_w.shape), _resident(conv_b.shape),
                  pl.BlockSpec((1, npre, 2 * D_FF), per_b), _resident(w_down.shape),
                  _resident(g.shape), _resident(b.shape)],
        out_specs=[blk, blk, pl.BlockSpec((1, SUBLANES, 2 * D_FF), per_b)],
        out_shape=[jax.ShapeDtypeStruct((B, S, D), F32), jax.ShapeDtypeStruct((B, S, D), BF16),
                   jax.ShapeDtypeStruct((B, SUBLANES, 2 * D_FF), F32)],
        scratch_shapes=[pltpu.VMEM((tm + SUBLANES, 2 * D_FF), F32), pltpu.VMEM((tm, D_FF), BF16)],
        compiler_params=_cparams(("parallel", "arbitrary")),
        name="convffn_ln",
    )(x_bf, x_f32, w_up, conv_w, conv_b, buf, w_down, g, b)


def _rope_tables(pos):
    half = DHA // 2
    inv = ROPE_THETA ** (-jnp.arange(half, dtype=F32) * 2.0 / DHA)
    ang = pos.astype(F32)[:, None] * inv[None, :]
    cos = jnp.tile(jnp.cos(ang), (1, LANES // half))
    sin = jnp.tile(jnp.sin(ang), (1, LANES // half))
    first = (jnp.arange(LANES) % DHA) < half
    return cos, jnp.where(first[None, :], -sin, sin)


def _split_weights(w_in):
    offs = [0]
    for s in IN_SIZES:
        offs.append(offs[-1] + s)
    part = lambda i: w_in[:, offs[i]:offs[i + 1]].astype(BF16)
    w_qkv = w_in[:, :offs[3]].astype(BF16)
    w_gif = jnp.pad(part(6), ((0, 0), (0, LANES - W_GIF)))
    return w_qkv, [part(3), part(4), part(5), part(7), part(8), w_gif]


MIX_DTYPES = [F32, BF16, F32, F32, F32, F32]


def _mixer_tail(x_f32, qkm, vm_bf, om, ga, gb, gif, oa, conv_m_buf, m_state, ffn_buf, p, cfg):
    B, S, _ = x_f32.shape
    g4 = gif[..., :W_GIF].reshape(B, S, 2, HM)
    gcol = g4.transpose(0, 3, 1, 2)
    grow = g4.transpose(0, 3, 2, 1)
    if m_state is None:
        c0 = jnp.zeros((B, HM, DKM, DVM), F32)
        n0 = jnp.zeros((B, HM, 1, DKM), F32)
        m0 = jnp.zeros((B, HM, 1, LANES), F32)
    else:
        c0 = m_state[0].astype(F32)
        n0 = m_state[1].astype(F32).reshape(B, HM, 1, DKM)
        m0 = jnp.broadcast_to(m_state[2].astype(F32)[:, :, None, None], (B, HM, 1, LANES))
    hm, c_new, n_new, m_new = _mlstm(p['b_if'].astype(F32), qkm, vm_bf, om, gcol, grow,
                                     p['conv_m_w'], p['conv_m_b'].reshape(1, W_QKM), conv_m_buf,
                                     c0, n0, m0, p['mh_g'].reshape(1, HM * DVM), cfg['lc'])
    new_conv_m = jnp.concatenate([conv_m_buf, qkm[:, S - min(S, CONV_M - 1):]], axis=1)[:, -(CONV_M - 1):]

    row = lambda a: a.reshape(1, -1)
    x1_f32, x1_bf = _merge_outproj(ga, oa, gb, hm, x_f32, p['w_out'].astype(BF16),
                                   row(p['ln1_g']), row(p['ln1_b']), cfg['tm_merge'])
    x2_f32, x2_bf, tail = _ffn(x1_bf, x1_f32, p['w_up'].astype(BF16), p['ffn_conv_w'], row(p['ffn_conv_b']),
                               ffn_buf, p['w_down'].astype(BF16), row(p['ln2_g']), row(p['ln2_b']),
                               cfg['tm_ffn'])
    new_ffn = tail[:, SUBLANES - (CONV_F - 1):]
    return (x2_f32, x2_bf), (new_conv_m, c_new, n_new.reshape(B, HM, DKM), m_new[:, :, 0, 0], new_ffn)


def _layer_prompt(x, layer, kv_all, p, lam_init, cfg):
    x_f32, x_mm = x
    B, S, _ = x_f32.shape
    w_qkv, w_mix = _split_weights(p['w_in'])
    cos, sin = _rope_tables(jnp.arange(S))
    qt, k_all, k_bf, v_all, vt = _qkv_proj_prompt(x_mm, w_qkv, cos, sin, cfg['tk'], layer, kv_all)
    qkm, vm_bf, om, ga, gb, gif = _proj(x_mm, w_mix, MIX_DTYPES, cfg['tm_proj'])
    oa = _attn_prompt(p['lam'].astype(F32), p['subln_g'].reshape(DVA, 1), qt, k_bf, vt, cfg['tq'], cfg['chunk'],
                      lam_init)
    conv_m0 = jnp.zeros((B, CONV_M - 1, W_QKM), F32)
    ffn0 = jnp.zeros((B, CONV_F - 1, 2 * D_FF), F32)
    x_new, st = _mixer_tail(x_f32, qkm, vm_bf, om, ga, gb, gif, oa, conv_m0, None, ffn0, p, cfg)
    return x_new, (k_all, v_all), st


def _layer_sample(x, layer, caches, conv_m_buf, m_state, ffn_buf, p, lam_init, cfg):
    x_f32, x_mm = x
    B, S, _ = x_f32.shape
    w_qkv, w_mix = _split_weights(p['w_in'])
    cos, sin = _rope_tables(PAST_LEN + jnp.arange(S))
    cos, sin = jnp.tile(cos, (B, 1)), jnp.tile(sin, (B, 1))
    xin = x_mm.reshape(1, B * S, D_MODEL)
    unflat = lambda a: a.reshape(B, S, a.shape[-1])
    q_bf, k_f32, v_f32 = map(unflat, _qkv_proj_sample(xin, w_qkv, cos, sin))
    qkm, vm_bf, om, ga, gb, gif = map(unflat, _proj(xin, w_mix, MIX_DTYPES, B * S))
    padn = ((0, 0), (0, LANES - S), (0, 0))
    oa = _attn_sample(p['lam'].astype(F32), p['subln_g'].reshape(1, DVA), q_bf, caches[0], caches[1],
                      jnp.pad(k_f32, padn), jnp.pad(v_f32, padn), S, lam_init, layer, cfg['tkv'])
    x_new, st = _mixer_tail(x_f32, qkm, vm_bf, om, ga, gb, gif, oa, conv_m_buf, m_state, ffn_buf, p, cfg)
    return x_new, (k_f32.reshape(B, S, HA, 2 * DHA), v_f32.reshape(B, S, HA, DVA)) + st


def _prompt_cfg(S):
    return dict(tm_proj=min(S, 512), tq=min(S, 1024), tk=min(S, 512), chunk=64, lc=min(S, 256),
                tm_merge=min(S, 512), tm_ffn=min(S, 512))


def _sample_cfg(S, P):
    return dict(lc=S, tm_merge=S, tm_ffn=S, tkv=min(P, 512))


def kernel(x_prompt, x_sample, cache_k, cache_v, state_mlstm_conv, state_mlstm_C, state_mlstm_n,
           state_mlstm_m, state_ffn_conv, w_in, b_if, mlstm_conv_w, mlstm_conv_b, diff_lambda,
           diff_subln_g, mlstm_norm_g, w_out, ln1_g, ln1_b, w_up, ffn_conv_w, ffn_conv_b, w_down,
           ln2_g, ln2_b):
    bp, sp, _ = x_prompt.shape
    bs, ss, _ = x_sample.shape
    past = cache_k.shape[2]
    xp = (x_prompt, x_prompt)
    xs = (x_sample, x_sample)
    caches = (cache_k.reshape(DEPTH, bs, past * HA, 2 * DHA), cache_v.reshape(DEPTH, bs, past * HA, DVA))
    kv_all = None
    st_p, st_s = [], []
    for l in range(DEPTH):
        p = {'w_in': w_in[l], 'b_if': b_if[l], 'conv_m_w': mlstm_conv_w[l], 'conv_m_b': mlstm_conv_b[l],
             'lam': diff_lambda[l], 'subln_g': diff_subln_g[l], 'mh_g': mlstm_norm_g[l], 'w_out': w_out[l],
             'ln1_g': ln1_g[l], 'ln1_b': ln1_b[l], 'w_up': w_up[l], 'ffn_conv_w': ffn_conv_w[l],
             'ffn_conv_b': ffn_conv_b[l], 'w_down': w_down[l], 'ln2_g': ln2_g[l], 'ln2_b': ln2_b[l]}
        lam_init = 0.8 - 0.6 * math.exp(-0.3 * l)
        xp, kv_all, sp_l = _layer_prompt(xp, l, kv_all, p, lam_init, _prompt_cfg(sp))
        xs, ss_l = _layer_sample(xs, l, caches, state_mlstm_conv[l],
                                 (state_mlstm_C[l], state_mlstm_n[l], state_mlstm_m[l]), state_ffn_conv[l],
                                 p, lam_init, _sample_cfg(ss, past))
        st_p.append(sp_l)
        st_s.append(ss_l)
    stk = lambda lst, i: jnp.stack([s[i] for s in lst])
    k_prompt = kv_all[0].reshape(DEPTH, bp, sp, HA, 2 * DHA)
    v_prompt = kv_all[1].reshape(DEPTH, bp, sp, HA, DVA)
    return ((xp[0], xs[0], k_prompt, v_prompt) + tuple(stk(st_p, i) for i in range(5))
            + tuple(stk(st_s, i) for i in range(7)))
```

```python
import functools
import math

import jax
import jax.numpy as jnp
from jax import lax
from jax.experimental import pallas as pl
from jax.experimental.pallas import tpu as pltpu

F32 = jnp.float32
BF16 = jnp.bfloat16

D_MODEL = 1024
DEPTH = 2
PAST_LEN = 4096
HA = 8
DHA = 64
DVA = 2 * DHA
HM = 4
DKM = 256
DVM = 256
CONV_M = 4
D_FF = 2816
CONV_F = 3
ROPE_THETA = 10000.0
LN_EPS = 1e-5
ALPHA = (2 * DEPTH) ** 0.25

W_QA = HA * 2 * DHA
W_KA = HA * 2 * DHA
W_VA = HA * DVA
W_QKM = 2 * HM * DKM
W_VM = HM * DVM
W_OM = HM * DVM
W_GIF = 2 * HM
W_GA = HA * DVA
W_GB = HM * DVM
IN_SIZES = (W_QA, W_KA, W_VA, W_QKM, W_VM, W_OM, W_GIF, W_GA, W_GB)

LANES = 128
SUBLANES = 8
NEG_BIG = -1e30
LOG2E = math.log2(math.e)
Q_SCALE = DHA ** -0.5 * LOG2E
EXP_SPAN = 64.0
NORM_SLACK = 1.05
VMEM_LIMIT = 52 * 1024 * 1024


def _cparams(sem):
    return pltpu.CompilerParams(dimension_semantics=sem, vmem_limit_bytes=VMEM_LIMIT)


def _resident(shape):
    nd = len(shape)
    return pl.BlockSpec(shape, lambda *_: (0,) * nd, pipeline_mode=pl.Buffered(1))


def _layernorm_rows(r, g, b):
    mu = jnp.mean(r, axis=-1, keepdims=True)
    d = r - mu
    var = jnp.mean(d * d, axis=-1, keepdims=True)
    return d * lax.rsqrt(var + LN_EPS) * g + b


def _sigmoid(x):
    return 0.5 * jnp.tanh(0.5 * x) + 0.5


def _log_sigmoid(x):
    return jnp.minimum(x, 0.0) - jnp.log1p(jnp.exp(-jnp.abs(x)))


def _rope_fn(cos_ref, sin_ref):
    cos = cos_ref[...]
    sin = sin_ref[...]
    lane = lax.broadcasted_iota(jnp.int32, cos.shape, 1)
    first = (lane % DHA) < (DHA // 2)

    def rope(z):
        rot = jnp.where(first, pltpu.roll(z, LANES - DHA // 2, 1), pltpu.roll(z, DHA // 2, 1))
        return z * cos + rot * sin

    return rope


def _qkv_dots(x, w_ref, c0, chunk):
    zq = jnp.dot(x, w_ref[:, c0:c0 + chunk], preferred_element_type=F32)
    zk = jnp.dot(x, w_ref[:, W_QA + c0:W_QA + c0 + chunk], preferred_element_type=F32)
    zv = jnp.dot(x, w_ref[:, W_QA + W_KA + c0:W_QA + W_KA + c0 + chunk], preferred_element_type=F32)
    return zq, zk, zv


def _qkv_prompt_kernel(x_ref, w_ref, cos_ref, sin_ref, *refs):
    qt_ref, kf_ref, kb_ref, vf_ref, vt_ref = refs[-5:]
    x = x_ref[0].astype(BF16)
    rope = _rope_fn(cos_ref, sin_ref)
    chunk = 512
    for c0 in range(0, W_QA, chunk):
        zq, zk, zv = _qkv_dots(x, w_ref, c0, chunk)
        for g0 in range(0, chunk, LANES):
            h = (c0 + g0) // LANES
            sl = slice(c0 + g0, c0 + g0 + LANES)
            q = rope(zq[:, g0:g0 + LANES]) * Q_SCALE
            k = rope(zk[:, g0:g0 + LANES])
            v = zv[:, g0:g0 + LANES]
            qt_ref[0, h, 0] = q.T.astype(BF16)
            rows = pl.ds(h, x.shape[0], stride=HA)
            kf_ref[0, 0, rows, :] = k
            kb_ref[0, :, sl] = k.astype(BF16)
            vf_ref[0, 0, rows, :] = v
            vt_ref[0, h, 0] = v.T.astype(BF16)


def _qkv_proj_prompt(x, w_qkv, cos, sin, tm, layer, kv_all):
    B, S, D = x.shape
    ns = S // tm
    row = lambda b, s: (b, s, 0)
    tab = lambda b, s: (s, 0)
    slot = lambda b, s: (layer, b, s, 0)
    tr = lambda b, s: (b, 0, s, 0, 0)
    in_specs = [pl.BlockSpec((1, tm, D), row), _resident(w_qkv.shape),
                pl.BlockSpec((tm, LANES), tab), pl.BlockSpec((tm, LANES), tab)]
    args = [x, w_qkv, cos, sin]
    aliases = {}
    if kv_all is not None:
        in_specs += [pl.BlockSpec(memory_space=pl.ANY)] * 2
        args += list(kv_all)
        aliases = {4: 1, 5: 3}
    t_shape = jax.ShapeDtypeStruct((B, HA, ns, DVA, tm), BF16)
    all_shape = jax.ShapeDtypeStruct((DEPTH, B, S * HA, DVA), F32)
    return pl.pallas_call(
        _qkv_prompt_kernel,
        grid=(B, ns),
        in_specs=in_specs,
        out_specs=[pl.BlockSpec((1, HA, 1, DVA, tm), tr), pl.BlockSpec((1, 1, tm * HA, DVA), slot),
                   pl.BlockSpec((1, tm, W_KA), row), pl.BlockSpec((1, 1, tm * HA, DVA), slot),
                   pl.BlockSpec((1, HA, 1, DVA, tm), tr)],
        out_shape=[t_shape, all_shape, jax.ShapeDtypeStruct((B, S, W_KA), BF16), all_shape, t_shape],
        input_output_aliases=aliases,
        compiler_params=_cparams(("parallel", "parallel")),
        name="qkv_rope_prompt",
    )(*args)


def _qkv_sample_kernel(x_ref, w_ref, cos_ref, sin_ref, q_ref, kf_ref, vf_ref):
    x = x_ref[0].astype(BF16)
    rope = _rope_fn(cos_ref, sin_ref)
    chunk = 512
    for c0 in range(0, W_QA, chunk):
        zq, zk, zv = _qkv_dots(x, w_ref, c0, chunk)
        for g0 in range(0, chunk, LANES):
            sl = slice(c0 + g0, c0 + g0 + LANES)
            q_ref[0, :, sl] = (rope(zq[:, g0:g0 + LANES]) * Q_SCALE).astype(BF16)
            kf_ref[0, :, sl] = rope(zk[:, g0:g0 + LANES])
        vf_ref[0, :, c0:c0 + chunk] = zv


def _qkv_proj_sample(x, w_qkv, cos, sin):
    B, M, D = x.shape
    out = lambda dt: jax.ShapeDtypeStruct((B, M, W_QA), dt)
    return pl.pallas_call(
        _qkv_sample_kernel,
        grid=(B,),
        in_specs=[pl.BlockSpec((1, M, D), lambda b: (b, 0, 0)), _resident(w_qkv.shape),
                  _resident(cos.shape), _resident(sin.shape)],
        out_specs=[pl.BlockSpec((1, M, W_QA), lambda b: (b, 0, 0))] * 3,
        out_shape=[out(BF16), out(F32), out(F32)],
        compiler_params=_cparams(("parallel",)),
        name="qkv_rope_sample",
    )(x, w_qkv, cos, sin)


def _proj_kernel(x_ref, *refs, n_out):
    x = x_ref[0].astype(BF16)
    chunk = 512
    for w_ref, o_ref in zip(refs[:n_out], refs[n_out:]):
        n = w_ref.shape[1]
        for c0 in range(0, n, chunk):
            c1 = min(c0 + chunk, n)
            o_ref[0, :, c0:c1] = jnp.dot(x, w_ref[:, c0:c1], preferred_element_type=F32).astype(o_ref.dtype)


def _proj(x, weights, dtypes, tm):
    B, S, D = x.shape
    row = lambda b, s: (b, s, 0)
    return pl.pallas_call(
        functools.partial(_proj_kernel, n_out=len(weights)),
        grid=(B, S // tm),
        in_specs=[pl.BlockSpec((1, tm, D), row)] + [_resident(w.shape) for w in weights],
        out_specs=[pl.BlockSpec((1, tm, w.shape[1]), row) for w in weights],
        out_shape=[jax.ShapeDtypeStruct((B, S, w.shape[1]), dt) for w, dt in zip(weights, dtypes)],
        compiler_params=_cparams(("parallel", "parallel")),
        name="mix_proj",
    )(x, *weights)


def _lambda(lp_ref, lam_init):
    lp = lp_ref[...]
    a = jnp.sum(lp[0:1] * lp[1:2], axis=1, keepdims=True)
    b = jnp.sum(lp[2:3] * lp[3:4], axis=1, keepdims=True)
    return jnp.exp(a) - jnp.exp(b) + lam_init


def _attn_prompt_kernel(lp_ref, g_ref, qt_ref, k_ref, vt_ref, o_ref, acc_ref, knorm_ref,
                        *, tq, tk, chunk, lam_init):
    qi = pl.program_id(2)
    r = tq // tk
    qt = jnp.concatenate([qt_ref[0, 0, t] for t in range(r)], axis=1)
    row = lax.broadcasted_iota(jnp.int32, qt.shape, 0)
    zero = jnp.zeros_like(qt)
    qq = jnp.concatenate([jnp.where(row < DHA, qt, zero), jnp.where(row >= DHA, qt, zero)], axis=1)
    acc_ref[...] = jnp.zeros_like(acc_ref)

    @pl.when(qi == 0)
    def _():
        kf = k_ref[0].astype(F32)
        dim = lax.broadcasted_iota(jnp.int32, (DVA, LANES), 0)
        col = lax.broadcasted_iota(jnp.int32, (DVA, LANES), 1)
        sel = jnp.where((dim // DHA) == col, 1.0, 0.0).astype(BF16)
        ksq = jnp.dot((kf * kf).astype(BF16), sel, preferred_element_type=F32)
        knorm_ref[...] = jnp.max(ksq, axis=0, keepdims=True)

    def both(x, c0):
        return x if c0 == 0 else jnp.concatenate([x[:, c0:tq], x[:, tq + c0:]], axis=1)

    def merge(old, new, c0):
        if c0 == 0:
            return new
        w = tq - c0
        return jnp.concatenate([old[:, :c0], new[:, :w], old[:, tq:tq + c0], new[:, w:]], axis=1)

    def scores(j, c0, mask):
        k = k_ref[0, pl.ds(pl.multiple_of(j * tk, tk), tk), :]
        s = jnp.dot(k, both(qq, c0), preferred_element_type=F32)
        if mask is not None:
            mk = mask[:, c0:]
            s = jnp.where(jnp.concatenate([mk, mk], axis=1), s, NEG_BIG)
        return s

    def accumulate(j, c0, p, scale):
        vt = vt_ref[0, 0, j]
        w = tq - c0
        pb = p.astype(BF16)
        for mp in range(2):
            pv = jnp.dot(vt, pb[:, mp * w:(mp + 1) * w], preferred_element_type=F32)
            old = acc_ref[mp, :, c0:]
            acc_ref[mp, :, c0:] = (old if scale is None else scale[:, mp * w:(mp + 1) * w] * old) + pv

    def running_max_step(j, carry, c0=0, mask=None):
        m, l = carry
        s = scores(j, c0, mask)
        mo = both(m, c0)
        n = jnp.maximum(mo, jnp.max(s, axis=0, keepdims=True))
        p = jnp.exp2(s - n)
        a = jnp.exp2(mo - n)
        ln = a * both(l, c0) + jnp.sum(p, axis=0, keepdims=True)
        accumulate(j, c0, p, a)
        return merge(m, n, c0), merge(l, ln, c0)

    def fixed_ref_step(j, carry, c0=0, mask=None):
        m, l = carry
        p = jnp.exp2(scores(j, c0, mask) - both(m, c0))
        accumulate(j, c0, p, None)
        return m, merge(l, both(l, c0) + jnp.sum(p, axis=0, keepdims=True), c0)

    nfull = qi * r

    def diag_mask(d):
        kr = (lax.broadcasted_iota(jnp.int32, (tk, tq), 0) + d * tk) // chunk
        qc = lax.broadcasted_iota(jnp.int32, (tk, tq), 1) // chunk
        return kr <= qc

    def sweep(step):
        def run(carry):
            for d in range(1, r):
                carry = step(nfull + d, carry, d * tk, diag_mask(d))
            return lax.fori_loop(0, nfull, step, carry)
        return run

    init = (jnp.full((1, 2 * tq), NEG_BIG, F32), jnp.zeros((1, 2 * tq), F32))
    carry = running_max_step(nfull, init, 0, diag_mask(0))

    qf = qt.astype(F32)
    qsq = qf * qf
    q1 = jnp.max(jnp.sum(qsq[:DHA], axis=0, keepdims=True), axis=1, keepdims=True)
    q2 = jnp.max(jnp.sum(qsq[DHA:], axis=0, keepdims=True), axis=1, keepdims=True)
    kn = knorm_ref[...]
    worst = jnp.maximum(q1 * kn[:, 0:1], q2 * kn[:, 1:2])
    safe = jnp.max(worst) * NORM_SLACK <= (0.5 * EXP_SPAN) ** 2
    _, l = lax.cond(safe, sweep(fixed_ref_step), sweep(running_max_step), carry)

    lam = _lambda(lp_ref, lam_init)
    o = acc_ref[0] / l[:, :tq] - lam * (acc_ref[1] / l[:, tq:])
    ms = jnp.mean(o * o, axis=0, keepdims=True)
    o = o * lax.rsqrt(ms + LN_EPS) * g_ref[...] * (1.0 - lam_init)
    o_ref[0] = o.T


def _attn_prompt(lp, g_col, qt, k_bf, vt, tq, chunk, lam_init):
    B, H, nt, _, tk = vt.shape
    S = k_bf.shape[1]
    r = tq // tk
    return pl.pallas_call(
        functools.partial(_attn_prompt_kernel, tq=tq, tk=tk, chunk=chunk, lam_init=lam_init),
        grid=(B, H, S // tq),
        in_specs=[_resident(lp.shape), _resident(g_col.shape),
                  pl.BlockSpec((1, 1, r, DVA, tk), lambda b, h, i: (b, h, i, 0, 0)),
                  pl.BlockSpec((1, S, DVA), lambda b, h, i: (b, 0, h)),
                  pl.BlockSpec((1, 1, nt, DVA, tk), lambda b, h, i: (b, h, 0, 0, 0))],
        out_specs=pl.BlockSpec((1, tq, DVA), lambda b, h, i: (b, i, h)),
        out_shape=jax.ShapeDtypeStruct((B, S, H * DVA), F32),
        scratch_shapes=[pltpu.VMEM((2, DVA, tq), F32), pltpu.VMEM((1, LANES), F32)],
        compiler_params=_cparams(("parallel", "parallel", "arbitrary")),
        name="diff_attn_prompt",
    )(lp, g_col, qt, k_bf, vt)


def _attn_sample_kernel(lp_ref, g_ref, q_ref, ck_ref, cv_ref, nk_ref, nv_ref, o_ref, m_s, l_s, acc_s,
                        *, tkv, n_new, lam_init):
    j = pl.program_id(1)
    L = q_ref.shape[1]
    nt = (((1,), (1,)), ((), ()))

    @pl.when(j == 0)
    def _():
        m_s[...] = jnp.full(m_s.shape, NEG_BIG, F32)
        l_s[...] = jnp.zeros_like(l_s)
        acc_s[...] = jnp.zeros_like(acc_s)

    def update(h, kf, vf, mask):
        q = q_ref[0, :, h * DVA:(h + 1) * DVA]
        lane = lax.broadcasted_iota(jnp.int32, q.shape, 1)
        zero = jnp.zeros_like(q)
        qq = jnp.concatenate([jnp.where(lane < DHA, q, zero), jnp.where(lane >= DHA, q, zero)], axis=0)
        s = lax.dot_general(qq, kf.astype(BF16), nt, preferred_element_type=F32)
        if mask is not None:
            s = jnp.where(mask, s, NEG_BIG)
        m = m_s[h]
        n = jnp.maximum(m, jnp.max(s, axis=1, keepdims=True))
        p = jnp.exp2(s - n)
        a = jnp.exp2(m - n)
        l_s[h] = a * l_s[h] + jnp.sum(p, axis=1, keepdims=True)
        acc_s[h] = a * acc_s[h] + jnp.dot(p.astype(BF16), vf.astype(BF16), preferred_element_type=F32)
        m_s[h] = n

    for h in range(HA):
        rows = pl.ds(h, tkv, stride=HA)
        update(h, ck_ref[0, 0, rows, :], cv_ref[0, 0, rows, :], None)

    @pl.when(j == pl.num_programs(1) - 1)
    def _():
        valid = lax.broadcasted_iota(jnp.int32, (2 * L, nk_ref.shape[1]), 1) < n_new
        lam = _lambda(lp_ref, lam_init)
        for h in range(HA):
            c = slice(h * DVA, (h + 1) * DVA)
            update(h, nk_ref[0, :, c], nv_ref[0, :, c], valid)
            o = acc_s[h] / l_s[h]
            o = o[:L] - lam * o[L:]
            ms = jnp.mean(o * o, axis=1, keepdims=True)
            o_ref[0, :, c] = o * lax.rsqrt(ms + LN_EPS) * g_ref[...] * (1.0 - lam_init)


def _attn_sample(lp, g_row, q_bf, ck_all, cv_all, nk, nv, n_new, lam_init, layer, tkv):
    B, L, _ = q_bf.shape
    rows = ck_all.shape[2]
    npad = nk.shape[1]
    per_b = lambda b, j: (b, 0, 0)
    blk = pl.BlockSpec((1, 1, tkv * HA, DVA), lambda b, j: (layer, b, j, 0))
    return pl.pallas_call(
        functools.partial(_attn_sample_kernel, tkv=tkv, n_new=n_new, lam_init=lam_init),
        grid=(B, rows // (tkv * HA)),
        in_specs=[_resident(lp.shape), _resident(g_row.shape),
                  pl.BlockSpec((1, L, HA * DVA), per_b), blk, blk,
                  pl.BlockSpec((1, npad, HA * DVA), per_b), pl.BlockSpec((1, npad, HA * DVA), per_b)],
        out_specs=pl.BlockSpec((1, L, HA * DVA), per_b),
        out_shape=jax.ShapeDtypeStruct((B, L, HA * DVA), F32),
        scratch_shapes=[pltpu.VMEM((HA, 2 * L, 1), F32), pltpu.VMEM((HA, 2 * L, 1), F32),
                        pltpu.VMEM((HA, 2 * L, DVA), F32)],
        compiler_params=_cparams(("parallel", "arbitrary")),
        name="diff_attn_sample",
    )(lp, g_row, q_bf, ck_all, cv_all, nk, nv)


def _mlstm_kernel(bif_ref, q_ref, k_ref, v_ref, om_ref, gc_ref, gr_ref, wq_ref, wk_ref, bq_ref, bk_ref,
                  cq_ref, ck_ref, c0_ref, n0_ref, m0_ref, mg_ref,
                  h_ref, c_ref, n_ref, m_ref, qpad, kpad, *, lc):
    h = pl.program_id(1)
    c = pl.program_id(2)
    npre = CONV_M - 1
    off = SUBLANES - npre

    @pl.when(c == 0)
    def _():
        c_ref[0, 0] = c0_ref[0, 0]
        n_ref[0, 0] = n0_ref[0, 0]
        m_ref[0, 0] = m0_ref[0, 0]
        qpad[0:SUBLANES, :] = jnp.zeros((SUBLANES, DKM), F32)
        kpad[0:SUBLANES, :] = jnp.zeros((SUBLANES, DKM), F32)
        qpad[off:SUBLANES, :] = cq_ref[0]
        kpad[off:SUBLANES, :] = ck_ref[0]

    qpad[SUBLANES:SUBLANES + lc, :] = q_ref[0]
    kpad[SUBLANES:SUBLANES + lc, :] = k_ref[0]

    def conv_silu(pad, w_ref, b_ref):
        xa = pad[...]
        y = b_ref[...]
        for j in range(CONV_M):
            xs = pltpu.roll(xa, lc + SUBLANES - (off + j), 0)
            y = y + w_ref[j:j + 1, :] * xs[0:lc, :]
        return y * _sigmoid(y)

    q = conv_silu(qpad, wq_ref, bq_ref)
    k = conv_silu(kpad, wk_ref, bk_ref) * (DKM ** -0.5)
    qpad[0:SUBLANES, :] = qpad[lc:lc + SUBLANES, :]
    kpad[0:SUBLANES, :] = kpad[lc:lc + SUBLANES, :]

    bi = bif_ref[h]
    bf = bif_ref[HM + h]
    gc = gc_ref[0, 0]
    gr = gr_ref[0, 0]
    ig_c = gc[:, 0:1] + bi
    lf_c = _log_sigmoid(gc[:, 1:2] + bf)
    ig_r = gr[0:1, :] + bi
    lf_r = _log_sigmoid(gr[1:2, :] + bf)

    t_i = lax.broadcasted_iota(jnp.int32, (lc, lc), 0)
    s_i = lax.broadcasted_iota(jnp.int32, (lc, lc), 1)
    causal = s_i <= t_i
    zeros = jnp.zeros((lc, lc), F32)
    bcum_c = jnp.sum(jnp.where(causal, lf_r, zeros), axis=1, keepdims=True)
    bcum_r = jnp.sum(jnp.where(t_i <= s_i, lf_c, zeros), axis=0, keepdims=True)

    m_prev = m_ref[0, 0][:, 0:1]
    d = jnp.where(causal, bcum_c - bcum_r + ig_r, NEG_BIG)
    inter = bcum_c + m_prev
    m_t = jnp.maximum(inter, jnp.max(d, axis=1, keepdims=True))
    w = jnp.exp(d - m_t)
    g = jnp.exp(inter - m_t)

    qb = q.astype(BF16)
    kb = k.astype(BF16)
    vb = v_ref[0]
    cmat = c_ref[0, 0]
    nrow = n_ref[0, 0]
    s = lax.dot_general(qb, kb, (((1,), (1,)), ((), ())), preferred_element_type=F32) * w
    num = (jnp.dot(s.astype(BF16), vb, preferred_element_type=F32)
           + g * jnp.dot(qb, cmat.astype(BF16), preferred_element_type=F32))
    den = jnp.sum(s, axis=1, keepdims=True) + g * jnp.sum(q * nrow, axis=1, keepdims=True)
    den = jnp.maximum(jnp.abs(den), jnp.exp(-m_t))
    hh = num / den

    m_end = m_t[lc - 1:lc, :]
    b_last = bcum_c[lc - 1:lc, :]
    w_end = jnp.exp(b_last - bcum_c + ig_c - m_end)
    g_end = jnp.exp(b_last + m_prev - m_end)
    kw = k * w_end
    c_ref[0, 0] = g_end * cmat + lax.dot_general(kw.astype(BF16), vb, (((0,), (0,)), ((), ())),
                                                 preferred_element_type=F32)
    n_ref[0, 0] = g_end * nrow + jnp.sum(kw, axis=0, keepdims=True)
    m_ref[0, 0] = jnp.broadcast_to(m_end, (1, LANES))

    mu = jnp.mean(hh, axis=1, keepdims=True)
    hc = hh - mu
    var = jnp.mean(hc * hc, axis=1, keepdims=True)
    h_ref[0] = hc * lax.rsqrt(var + LN_EPS) * mg_ref[...] * _sigmoid(om_ref[0])


def _mlstm(b_if, qkm, vm_bf, om, gcol, grow, conv_w, conv_b, conv_state, c0, n0, m0, mh_g, lc):
    B, S, _ = qkm.shape
    nc = S // lc
    npre = CONV_M - 1
    qcol = lambda b, h, c: (b, c, h)
    kcol = lambda b, h, c: (b, c, HM + h)
    st = lambda b, h, c: (b, h, 0, 0)
    return pl.pallas_call(
        functools.partial(_mlstm_kernel, lc=lc),
        grid=(B, HM, nc),
        in_specs=[pl.BlockSpec(memory_space=pltpu.SMEM),
                  pl.BlockSpec((1, lc, DKM), qcol), pl.BlockSpec((1, lc, DKM), kcol),
                  pl.BlockSpec((1, lc, DVM), qcol), pl.BlockSpec((1, lc, DVM), qcol),
                  pl.BlockSpec((1, 1, lc, 2), lambda b, h, c: (b, h, c, 0)),
                  pl.BlockSpec((1, 1, 2, lc), lambda b, h, c: (b, h, 0, c)),
                  pl.BlockSpec((CONV_M, DKM), lambda b, h, c: (0, h)),
                  pl.BlockSpec((CONV_M, DKM), lambda b, h, c: (0, HM + h)),
                  pl.BlockSpec((1, DKM), lambda b, h, c: (0, h)),
                  pl.BlockSpec((1, DKM), lambda b, h, c: (0, HM + h)),
                  pl.BlockSpec((1, npre, DKM), lambda b, h, c: (b, 0, h)),
                  pl.BlockSpec((1, npre, DKM), lambda b, h, c: (b, 0, HM + h)),
                  pl.BlockSpec((1, 1, DKM, DVM), st), pl.BlockSpec((1, 1, 1, DKM), st),
                  pl.BlockSpec((1, 1, 1, LANES), st),
                  pl.BlockSpec((1, DVM), lambda b, h, c: (0, h))],
        out_specs=[pl.BlockSpec((1, lc, DVM), qcol),
                   pl.BlockSpec((1, 1, DKM, DVM), st), pl.BlockSpec((1, 1, 1, DKM), st),
                   pl.BlockSpec((1, 1, 1, LANES), st)],
        out_shape=[jax.ShapeDtypeStruct((B, S, HM * DVM), F32),
                   jax.ShapeDtypeStruct((B, HM, DKM, DVM), F32),
                   jax.ShapeDtypeStruct((B, HM, 1, DKM), F32),
                   jax.ShapeDtypeStruct((B, HM, 1, LANES), F32)],
        scratch_shapes=[pltpu.VMEM((lc + SUBLANES, DKM), F32), pltpu.VMEM((lc + SUBLANES, DKM), F32)],
        compiler_params=_cparams(("parallel", "parallel", "arbitrary")),
        name="mlstm",
    )(b_if, qkm, qkm, vm_bf, om, gcol, grow, conv_w, conv_w, conv_b, conv_b, conv_state, conv_state,
      c0, n0, m0, mh_g)


def _merge_kernel(ga_ref, oa_ref, gb_ref, hm_ref, x_ref, w_ref, g_ref, b_ref, xf_ref, xb_ref):
    y = _sigmoid(ga_ref[0]) * oa_ref[0] + _sigmoid(gb_ref[0]) * hm_ref[0]
    z = jnp.dot(y.astype(BF16), w_ref[...], preferred_element_type=F32)
    out = _layernorm_rows(ALPHA * x_ref[0] + z, g_ref[...], b_ref[...])
    xf_ref[0] = out
    xb_ref[0] = out.astype(BF16)


def _merge_outproj(ga, oa, gb, hm, x, w_out, g, b, tm):
    B, S, D = x.shape
    row = lambda bb, s: (bb, s, 0)
    blk = pl.BlockSpec((1, tm, D), row)
    return pl.pallas_call(
        _merge_kernel,
        grid=(B, S // tm),
        in_specs=[blk, blk, blk, blk, blk, _resident(w_out.shape), _resident(g.shape), _resident(b.shape)],
        out_specs=[blk, blk],
        out_shape=[jax.ShapeDtypeStruct((B, S, D), F32), jax.ShapeDtypeStruct((B, S, D), BF16)],
        compiler_params=_cparams(("parallel", "parallel")),
        name="merge_outproj_ln",
    )(ga, oa, gb, hm, x, w_out, g, b)


def _ffn_kernel(xb_ref, xf_ref, wu_ref, cw_ref, cb_ref, buf_ref, wd_ref, g_ref, b_ref,
                of_ref, ob_ref, tail_ref, upad, hbuf, *, tm):
    npre = CONV_F - 1
    off = SUBLANES - npre

    @pl.when(pl.program_id(1) == 0)
    def _():
        upad[0:SUBLANES, :] = jnp.zeros((SUBLANES, 2 * D_FF), F32)
        upad[off:SUBLANES, :] = buf_ref[0]

    x = xb_ref[0]
    chunk = 512
    for c0 in range(0, 2 * D_FF, chunk):
        upad[SUBLANES:SUBLANES + tm, c0:c0 + chunk] = jnp.dot(
            x, wu_ref[:, c0:c0 + chunk], preferred_element_type=F32)

    def conv(c0, c1):
        y = cb_ref[:, c0:c1]
        for j in range(CONV_F):
            y = y + cw_ref[j:j + 1, c0:c1] * upad[off + j:off + j + tm, c0:c1]
        return y

    hchunk = 256
    for c0 in range(0, D_FF, hchunk):
        a = conv(c0, c0 + hchunk)
        gate = conv(D_FF + c0, D_FF + c0 + hchunk)
        act = 0.5 * a * (1.0 + lax.erf(a * (2.0 ** -0.5)))
        hbuf[:, c0:c0 + hchunk] = (act * gate).astype(BF16)

    tail = upad[tm:tm + SUBLANES, :]
    tail_ref[0] = tail
    upad[0:SUBLANES, :] = tail

    z = jnp.dot(hbuf[...], wd_ref[...], preferred_element_type=F32)
    out = _layernorm_rows(ALPHA * xf_ref[0] + z, g_ref[...], b_ref[...])
    of_ref[0] = out
    ob_ref[0] = out.astype(BF16)


def _ffn(x_bf, x_f32, w_up, conv_w, conv_b, buf, w_down, g, b, tm):
    B, S, D = x_f32.shape
    npre = CONV_F - 1
    row = lambda bb, s: (bb, s, 0)
    blk = pl.BlockSpec((1, tm, D), row)
    per_b = lambda bb, s: (bb, 0, 0)
    return pl.pallas_call(
        functools.partial(_ffn_kernel, tm=tm),
        grid=(B, S // tm),
        in_specs=[blk, blk, _resident(w_up.shape), _resident(conv_w.shape), _resident(conv_b.shape),
                  pl.BlockSpec((1, npre, 2 * D_FF), per_b), _resident(w_down.shape),
                  _resident(g.shape), _resident(b.shape)],
        out_specs=[blk, blk, pl.BlockSpec((1, SUBLANES, 2 * D_FF), per_b)],
        out_shape=[jax.ShapeDtypeStruct((B, S, D), F32), jax.ShapeDtypeStruct((B, S, D), BF16),
                   jax.ShapeDtypeStruct((B, SUBLANES, 2 * D_FF), F32)],
        scratch_shapes=[pltpu.VMEM((tm + SUBLANES, 2 * D_FF), F32), pltpu.VMEM((tm, D_FF), BF16)],
        compiler_params=_cparams(("parallel", "arbitrary")),
        name="convffn_ln",
    )(x_bf, x_f32, w_up, conv_w, conv_b, buf, w_down, g, b)


def _rope_tables(pos):
    half = DHA // 2
    inv = ROPE_THETA ** (-jnp.arange(half, dtype=F32) * 2.0 / DHA)
    ang = pos.astype(F32)[:, None] * inv[None, :]
    cos = jnp.tile(jnp.cos(ang), (1, LANES // half))
    sin = jnp.tile(jnp.sin(ang), (1, LANES // half))
    first = (jnp.arange(LANES) % DHA) < half
    return cos, jnp.where(first[None, :], -sin, sin)


def _split_weights(w_in):
    offs = [0]
    for s in IN_SIZES:
        offs.append(offs[-1] + s)
    part = lambda i: w_in[:, offs[i]:offs[i + 1]].astype(BF16)
    w_qkv = w_in[:, :offs[3]].astype(BF16)
    w_gif = jnp.pad(part(6), ((0, 0), (0, LANES - W_GIF)))
    return w_qkv, [part(3), part(4), part(5), part(7), part(8), w_gif]


MIX_DTYPES = [F32, BF16, F32, F32, F32, F32]


def _mixer_tail(x_f32, qkm, vm_bf, om, ga, gb, gif, oa, conv_m_buf, m_state, ffn_buf, p, cfg):
    B, S, _ = x_f32.shape
    g4 = gif[..., :W_GIF].reshape(B, S, 2, HM)
    gcol = g4.transpose(0, 3, 1, 2)
    grow = g4.transpose(0, 3, 2, 1)
    if m_state is None:
        c0 = jnp.zeros((B, HM, DKM, DVM), F32)
        n0 = jnp.zeros((B, HM, 1, DKM), F32)
        m0 = jnp.zeros((B, HM, 1, LANES), F32)
    else:
        c0 = m_state[0].astype(F32)
        n0 = m_state[1].astype(F32).reshape(B, HM, 1, DKM)
        m0 = jnp.broadcast_to(m_state[2].astype(F32)[:, :, None, None], (B, HM, 1, LANES))
    hm, c_new, n_new, m_new = _mlstm(p['b_if'].astype(F32), qkm, vm_bf, om, gcol, grow,
                                     p['conv_m_w'], p['conv_m_b'].reshape(1, W_QKM), conv_m_buf,
                                     c0, n0, m0, p['mh_g'].reshape(1, HM * DVM), cfg['lc'])
    new_conv_m = jnp.concatenate([conv_m_buf, qkm[:, S - min(S, CONV_M - 1):]], axis=1)[:, -(CONV_M - 1):]

    row = lambda a: a.reshape(1, -1)
    x1_f32, x1_bf = _merge_outproj(ga, oa, gb, hm, x_f32, p['w_out'].astype(BF16),
                                   row(p['ln1_g']), row(p['ln1_b']), cfg['tm_merge'])
    x2_f32, x2_bf, tail = _ffn(x1_bf, x1_f32, p['w_up'].astype(BF16), p['ffn_conv_w'], row(p['ffn_conv_b']),
                               ffn_buf, p['w_down'].astype(BF16), row(p['ln2_g']), row(p['ln2_b']),
                               cfg['tm_ffn'])
    new_ffn = tail[:, SUBLANES - (CONV_F - 1):]
    return (x2_f32, x2_bf), (new_conv_m, c_new, n_new.reshape(B, HM, DKM), m_new[:, :, 0, 0], new_ffn)


def _layer_prompt(x, layer, kv_all, p, lam_init, cfg):
    x_f32, x_mm = x
    B, S, _ = x_f32.shape
    w_qkv, w_mix = _split_weights(p['w_in'])
    cos, sin = _rope_tables(jnp.arange(S))
    qt, k_all, k_bf, v_all, vt = _qkv_proj_prompt(x_mm, w_qkv, cos, sin, cfg['tk'], layer, kv_all)
    qkm, vm_bf, om, ga, gb, gif = _proj(x_mm, w_mix, MIX_DTYPES, cfg['tm_proj'])
    oa = _attn_prompt(p['lam'].astype(F32), p['subln_g'].reshape(DVA, 1), qt, k_bf, vt, cfg['tq'], cfg['chunk'],
                      lam_init)
    conv_m0 = jnp.zeros((B, CONV_M - 1, W_QKM), F32)
    ffn0 = jnp.zeros((B, CONV_F - 1, 2 * D_FF), F32)
    x_new, st = _mixer_tail(x_f32, qkm, vm_bf, om, ga, gb, gif, oa, conv_m0, None, ffn0, p, cfg)
    return x_new, (k_all, v_all), st


def _layer_sample(x, layer, caches, conv_m_buf, m_state, ffn_buf, p, lam_init, cfg):
    x_f32, x_mm = x
    B, S, _ = x_f32.shape
    w_qkv, w_mix = _split_weights(p['w_in'])
    cos, sin = _rope_tables(PAST_LEN + jnp.arange(S))
    cos, sin = jnp.tile(cos, (B, 1)), jnp.tile(sin, (B, 1))
    xin = x_mm.reshape(1, B * S, D_MODEL)
    unflat = lambda a: a.reshape(B, S, a.shape[-1])
    q_bf, k_f32, v_f32 = map(unflat, _qkv_proj_sample(xin, w_qkv, cos, sin))
    qkm, vm_bf, om, ga, gb, gif = map(unflat, _proj(xin, w_mix, MIX_DTYPES, B * S))
    padn = ((0, 0), (0, LANES - S), (0, 0))
    oa = _attn_sample(p['lam'].astype(F32), p['subln_g'].reshape(1, DVA), q_bf, caches[0], caches[1],
                      jnp.pad(k_f32, padn), jnp.pad(v_f32, padn), S, lam_init, layer, cfg['tkv'])
    x_new, st = _mixer_tail(x_f32, qkm, vm_bf, om, ga, gb, gif, oa, conv_m_buf, m_state, ffn_buf, p, cfg)
    return x_new, (k_f32.reshape(B, S, HA, 2 * DHA), v_f32.reshape(B, S, HA, DVA)) + st


def _prompt_cfg(S):
    return dict(tm_proj=min(S, 512), tq=min(S, 2048), tk=min(S, 512), chunk=64, lc=min(S, 256),
                tm_merge=min(S, 512), tm_ffn=min(S, 512))


def _sample_cfg(S, P):
    return dict(lc=S, tm_merge=S, tm_ffn=S, tkv=min(P, 512))


def kernel(x_prompt, x_sample, cache_k, cache_v, state_mlstm_conv, state_mlstm_C, state_mlstm_n,
           state_mlstm_m, state_ffn_conv, w_in, b_if, mlstm_conv_w, mlstm_conv_b, diff_lambda,
           diff_subln_g, mlstm_norm_g, w_out, ln1_g, ln1_b, w_up, ffn_conv_w, ffn_conv_b, w_down,
           ln2_g, ln2_b):
    bp, sp, _ = x_prompt.shape
    bs, ss, _ = x_sample.shape
    past = cache_k.shape[2]
    xp = (x_prompt, x_prompt)
    xs = (x_sample, x_sample)
    caches = (cache_k.reshape(DEPTH, bs, past * HA, 2 * DHA), cache_v.reshape(DEPTH, bs, past * HA, DVA))
    kv_all = None
    st_p, st_s = [], []
    for l in range(DEPTH):
        p = {'w_in': w_in[l], 'b_if': b_if[l], 'conv_m_w': mlstm_conv_w[l], 'conv_m_b': mlstm_conv_b[l],
             'lam': diff_lambda[l], 'subln_g': diff_subln_g[l], 'mh_g': mlstm_norm_g[l], 'w_out': w_out[l],
             'ln1_g': ln1_g[l], 'ln1_b': ln1_b[l], 'w_up': w_up[l], 'ffn_conv_w': ffn_conv_w[l],
             'ffn_conv_b': ffn_conv_b[l], 'w_down': w_down[l], 'ln2_g': ln2_g[l], 'ln2_b': ln2_b[l]}
        lam_init = 0.8 - 0.6 * math.exp(-0.3 * l)
        xp, kv_all, sp_l = _layer_prompt(xp, l, kv_all, p, lam_init, _prompt_cfg(sp))
        xs, ss_l = _layer_sample(xs, l, caches, state_mlstm_conv[l],
                                 (state_mlstm_C[l], state_mlstm_n[l], state_mlstm_m[l]), state_ffn_conv[l],
                                 p, lam_init, _sample_cfg(ss, past))
        st_p.append(sp_l)
        st_s.append(ss_l)
    stk = lambda lst, i: jnp.stack([s[i] for s in lst])
    k_prompt = kv_all[0].reshape(DEPTH, bp, sp, HA, 2 * DHA)
    v_prompt = kv_all[1].reshape(DEPTH, bp, sp, HA, DVA)
    return ((xp[0], xs[0], k_prompt, v_prompt) + tuple(stk(st_p, i) for i in range(5))
            + tuple(stk(st_s, i) for i in range(7)))
```

```python
import functools
import math

import jax
import jax.numpy as jnp
from jax import lax
from jax.experimental import pallas as pl
from jax.experimental.pallas import tpu as pltpu

F32 = jnp.float32
BF16 = jnp.bfloat16

D_MODEL = 1024
DEPTH = 2
PAST_LEN = 4096
HA = 8
DHA = 64
DVA = 2 * DHA
HM = 4
DKM = 256
DVM = 256
CONV_M = 4
D_FF = 2816
CONV_F = 3
ROPE_THETA = 10000.0
LN_EPS = 1e-5
ALPHA = (2 * DEPTH) ** 0.25

W_QA = HA * 2 * DHA
W_KA = HA * 2 * DHA
W_VA = HA * DVA
W_QKM = 2 * HM * DKM
W_VM = HM * DVM
W_OM = HM * DVM
W_GIF = 2 * HM
W_GA = HA * DVA
W_GB = HM * DVM
IN_SIZES = (W_QA, W_KA, W_VA, W_QKM, W_VM, W_OM, W_GIF, W_GA, W_GB)

LANES = 128
SUBLANES = 8
NEG_BIG = -1e30
LOG2E = math.log2(math.e)
Q_SCALE = DHA ** -0.5 * LOG2E
EXP_SPAN = 64.0
NORM_SLACK = 1.05
VMEM_LIMIT = 52 * 1024 * 1024


def _cparams(sem):
    return pltpu.CompilerParams(dimension_semantics=sem, vmem_limit_bytes=VMEM_LIMIT)


def _resident(shape):
    nd = len(shape)
    return pl.BlockSpec(shape, lambda *_: (0,) * nd, pipeline_mode=pl.Buffered(1))


def _layernorm_rows(r, g, b):
    mu = jnp.mean(r, axis=-1, keepdims=True)
    d = r - mu
    var = jnp.mean(d * d, axis=-1, keepdims=True)
    return d * lax.rsqrt(var + LN_EPS) * g + b


def _sigmoid(x):
    return 0.5 * jnp.tanh(0.5 * x) + 0.5


def _log_sigmoid(x):
    return jnp.minimum(x, 0.0) - jnp.log1p(jnp.exp(-jnp.abs(x)))


def _rope_fn(cos_ref, sin_ref):
    cos = cos_ref[...]
    sin = sin_ref[...]
    lane = lax.broadcasted_iota(jnp.int32, cos.shape, 1)
    first = (lane % DHA) < (DHA // 2)

    def rope(z):
        rot = jnp.where(first, pltpu.roll(z, LANES - DHA // 2, 1), pltpu.roll(z, DHA // 2, 1))
        return z * cos + rot * sin

    return rope


def _qkv_dots(x, w_ref, c0, chunk):
    zq = jnp.dot(x, w_ref[:, c0:c0 + chunk], preferred_element_type=F32)
    zk = jnp.dot(x, w_ref[:, W_QA + c0:W_QA + c0 + chunk], preferred_element_type=F32)
    zv = jnp.dot(x, w_ref[:, W_QA + W_KA + c0:W_QA + W_KA + c0 + chunk], preferred_element_type=F32)
    return zq, zk, zv


def _qkv_prompt_kernel(x_ref, w_ref, cos_ref, sin_ref, *refs):
    qt_ref, kf_ref, kb_ref, vf_ref, vt_ref = refs[-5:]
    x = x_ref[0].astype(BF16)
    rope = _rope_fn(cos_ref, sin_ref)
    chunk = 512
    for c0 in range(0, W_QA, chunk):
        zq, zk, zv = _qkv_dots(x, w_ref, c0, chunk)
        for g0 in range(0, chunk, LANES):
            h = (c0 + g0) // LANES
            sl = slice(c0 + g0, c0 + g0 + LANES)
            q = rope(zq[:, g0:g0 + LANES]) * Q_SCALE
            k = rope(zk[:, g0:g0 + LANES])
            v = zv[:, g0:g0 + LANES]
            qt_ref[0, h, 0] = q.T.astype(BF16)
            rows = pl.ds(h, x.shape[0], stride=HA)
            kf_ref[0, 0, rows, :] = k
            kb_ref[0, :, sl] = k.astype(BF16)
            vf_ref[0, 0, rows, :] = v
            vt_ref[0, h, 0] = v.T.astype(BF16)


def _qkv_proj_prompt(x, w_qkv, cos, sin, tm, layer, kv_all):
    B, S, D = x.shape
    ns = S // tm
    row = lambda b, s: (b, s, 0)
    tab = lambda b, s: (s, 0)
    slot = lambda b, s: (layer, b, s, 0)
    tr = lambda b, s: (b, 0, s, 0, 0)
    in_specs = [pl.BlockSpec((1, tm, D), row), _resident(w_qkv.shape),
                pl.BlockSpec((tm, LANES), tab), pl.BlockSpec((tm, LANES), tab)]
    args = [x, w_qkv, cos, sin]
    aliases = {}
    if kv_all is not None:
        in_specs += [pl.BlockSpec(memory_space=pl.ANY)] * 2
        args += list(kv_all)
        aliases = {4: 1, 5: 3}
    t_shape = jax.ShapeDtypeStruct((B, HA, ns, DVA, tm), BF16)
    all_shape = jax.ShapeDtypeStruct((DEPTH, B, S * HA, DVA), F32)
    return pl.pallas_call(
        _qkv_prompt_kernel,
        grid=(B, ns),
        in_specs=in_specs,
        out_specs=[pl.BlockSpec((1, HA, 1, DVA, tm), tr), pl.BlockSpec((1, 1, tm * HA, DVA), slot),
                   pl.BlockSpec((1, tm, W_KA), row), pl.BlockSpec((1, 1, tm * HA, DVA), slot),
                   pl.BlockSpec((1, HA, 1, DVA, tm), tr)],
        out_shape=[t_shape, all_shape, jax.ShapeDtypeStruct((B, S, W_KA), BF16), all_shape, t_shape],
        input_output_aliases=aliases,
        compiler_params=_cparams(("parallel", "parallel")),
        name="qkv_rope_prompt",
    )(*args)


def _qkv_sample_kernel(x_ref, w_ref, cos_ref, sin_ref, q_ref, kf_ref, vf_ref):
    x = x_ref[0].astype(BF16)
    rope = _rope_fn(cos_ref, sin_ref)
    chunk = 512
    for c0 in range(0, W_QA, chunk):
        zq, zk, zv = _qkv_dots(x, w_ref, c0, chunk)
        for g0 in range(0, chunk, LANES):
            sl = slice(c0 + g0, c0 + g0 + LANES)
            q_ref[0, :, sl] = (rope(zq[:, g0:g0 + LANES]) * Q_SCALE).astype(BF16)
            kf_ref[0, :, sl] = rope(zk[:, g0:g0 + LANES])
        vf_ref[0, :, c0:c0 + chunk] = zv


def _qkv_proj_sample(x, w_qkv, cos, sin):
    B, M, D = x.shape
    out = lambda dt: jax.ShapeDtypeStruct((B, M, W_QA), dt)
    return pl.pallas_call(
        _qkv_sample_kernel,
        grid=(B,),
        in_specs=[pl.BlockSpec((1, M, D), lambda b: (b, 0, 0)), _resident(w_qkv.shape),
                  _resident(cos.shape), _resident(sin.shape)],
        out_specs=[pl.BlockSpec((1, M, W_QA), lambda b: (b, 0, 0))] * 3,
        out_shape=[out(BF16), out(F32), out(F32)],
        compiler_params=_cparams(("parallel",)),
        name="qkv_rope_sample",
    )(x, w_qkv, cos, sin)


def _proj_kernel(x_ref, *refs, n_out):
    x = x_ref[0].astype(BF16)
    chunk = 512
    for w_ref, o_ref in zip(refs[:n_out], refs[n_out:]):
        n = w_ref.shape[1]
        for c0 in range(0, n, chunk):
            c1 = min(c0 + chunk, n)
            o_ref[0, :, c0:c1] = jnp.dot(x, w_ref[:, c0:c1], preferred_element_type=F32).astype(o_ref.dtype)


def _proj(x, weights, dtypes, tm):
    B, S, D = x.shape
    row = lambda b, s: (b, s, 0)
    return pl.pallas_call(
        functools.partial(_proj_kernel, n_out=len(weights)),
        grid=(B, S // tm),
        in_specs=[pl.BlockSpec((1, tm, D), row)] + [_resident(w.shape) for w in weights],
        out_specs=[pl.BlockSpec((1, tm, w.shape[1]), row) for w in weights],
        out_shape=[jax.ShapeDtypeStruct((B, S, w.shape[1]), dt) for w, dt in zip(weights, dtypes)],
        compiler_params=_cparams(("parallel", "parallel")),
        name="mix_proj",
    )(x, *weights)


def _lambda(lp_ref, lam_init):
    lp = lp_ref[...]
    a = jnp.sum(lp[0:1] * lp[1:2], axis=1, keepdims=True)
    b = jnp.sum(lp[2:3] * lp[3:4], axis=1, keepdims=True)
    return jnp.exp(a) - jnp.exp(b) + lam_init


def _attn_prompt_kernel(lp_ref, g_ref, qt_ref, k_ref, vt_ref, o_ref, acc_ref, knorm_ref,
                        *, tq, tk, chunk, lam_init):
    qi = pl.program_id(2)
    r = tq // tk
    qt = jnp.concatenate([qt_ref[0, 0, t] for t in range(r)], axis=1)
    row = lax.broadcasted_iota(jnp.int32, qt.shape, 0)
    zero = jnp.zeros_like(qt)
    qq = jnp.concatenate([jnp.where(row < DHA, qt, zero), jnp.where(row >= DHA, qt, zero)], axis=1)
    acc_ref[...] = jnp.zeros_like(acc_ref)

    @pl.when(qi == 0)
    def _():
        kf = k_ref[0].astype(F32)
        dim = lax.broadcasted_iota(jnp.int32, (DVA, LANES), 0)
        col = lax.broadcasted_iota(jnp.int32, (DVA, LANES), 1)
        sel = jnp.where((dim // DHA) == col, 1.0, 0.0).astype(BF16)
        ksq = jnp.dot((kf * kf).astype(BF16), sel, preferred_element_type=F32)
        knorm_ref[...] = jnp.max(ksq, axis=0, keepdims=True)

    def both(x, c0):
        return x if c0 == 0 else jnp.concatenate([x[:, c0:tq], x[:, tq + c0:]], axis=1)

    def merge(old, new, c0):
        if c0 == 0:
            return new
        w = tq - c0
        return jnp.concatenate([old[:, :c0], new[:, :w], old[:, tq:tq + c0], new[:, w:]], axis=1)

    def scores(j, c0, mask):
        k = k_ref[0, pl.ds(pl.multiple_of(j * tk, tk), tk), :]
        s = jnp.dot(k, both(qq, c0), preferred_element_type=F32)
        if mask is not None:
            mk = mask[:, c0:]
            s = jnp.where(jnp.concatenate([mk, mk], axis=1), s, NEG_BIG)
        return s

    def accumulate(j, c0, p, scale):
        vt = vt_ref[0, 0, j]
        w = tq - c0
        pb = p.astype(BF16)
        for mp in range(2):
            pv = jnp.dot(vt, pb[:, mp * w:(mp + 1) * w], preferred_element_type=F32)
            old = acc_ref[mp, :, c0:]
            acc_ref[mp, :, c0:] = (old if scale is None else scale[:, mp * w:(mp + 1) * w] * old) + pv

    def running_max_step(j, carry, c0=0, mask=None):
        m, l = carry
        s = scores(j, c0, mask)
        mo = both(m, c0)
        n = jnp.maximum(mo, jnp.max(s, axis=0, keepdims=True))
        p = jnp.exp2(s - n)
        a = jnp.exp2(mo - n)
        ln = a * both(l, c0) + jnp.sum(p, axis=0, keepdims=True)
        accumulate(j, c0, p, a)
        return merge(m, n, c0), merge(l, ln, c0)

    def fixed_ref_step(j, carry, c0=0, mask=None):
        m, l = carry
        p = jnp.exp2(scores(j, c0, mask) - both(m, c0))
        accumulate(j, c0, p, None)
        return m, merge(l, both(l, c0) + jnp.sum(p, axis=0, keepdims=True), c0)

    nfull = qi * r

    def diag_mask(d):
        kr = (lax.broadcasted_iota(jnp.int32, (tk, tq), 0) + d * tk) // chunk
        qc = lax.broadcasted_iota(jnp.int32, (tk, tq), 1) // chunk
        return kr <= qc

    def sweep(step):
        def run(carry):
            for d in range(1, r):
                carry = step(nfull + d, carry, d * tk, diag_mask(d))
            return lax.fori_loop(0, nfull, step, carry)
        return run

    init = (jnp.full((1, 2 * tq), NEG_BIG, F32), jnp.zeros((1, 2 * tq), F32))
    carry = running_max_step(nfull, init, 0, diag_mask(0))

    qf = qt.astype(F32)
    qsq = qf * qf
    q1 = jnp.max(jnp.sum(qsq[:DHA], axis=0, keepdims=True), axis=1, keepdims=True)
    q2 = jnp.max(jnp.sum(qsq[DHA:], axis=0, keepdims=True), axis=1, keepdims=True)
    kn = knorm_ref[...]
    worst = jnp.maximum(q1 * kn[:, 0:1], q2 * kn[:, 1:2])
    safe = jnp.max(worst) * NORM_SLACK <= (0.5 * EXP_SPAN) ** 2
    _, l = lax.cond(safe, sweep(fixed_ref_step), sweep(running_max_step), carry)

    lam = _lambda(lp_ref, lam_init)
    o = acc_ref[0] / l[:, :tq] - lam * (acc_ref[1] / l[:, tq:])
    ms = jnp.mean(o * o, axis=0, keepdims=True)
    o = o * lax.rsqrt(ms + LN_EPS) * g_ref[...] * (1.0 - lam_init)
    o_ref[0] = o.T


def _attn_prompt(lp, g_col, qt, k_bf, vt, tq, chunk, lam_init):
    B, H, nt, _, tk = vt.shape
    S = k_bf.shape[1]
    r = tq // tk
    return pl.pallas_call(
        functools.partial(_attn_prompt_kernel, tq=tq, tk=tk, chunk=chunk, lam_init=lam_init),
        grid=(B, H, S // tq),
        in_specs=[_resident(lp.shape), _resident(g_col.shape),
                  pl.BlockSpec((1, 1, r, DVA, tk), lambda b, h, i: (b, h, i, 0, 0)),
                  pl.BlockSpec((1, S, DVA), lambda b, h, i: (b, 0, h)),
                  pl.BlockSpec((1, 1, nt, DVA, tk), lambda b, h, i: (b, h, 0, 0, 0))],
        out_specs=pl.BlockSpec((1, tq, DVA), lambda b, h, i: (b, i, h)),
        out_shape=jax.ShapeDtypeStruct((B, S, H * DVA), F32),
        scratch_shapes=[pltpu.VMEM((2, DVA, tq), F32), pltpu.VMEM((1, LANES), F32)],
        compiler_params=_cparams(("parallel", "parallel", "arbitrary")),
        name="diff_attn_prompt",
    )(lp, g_col, qt, k_bf, vt)


def _attn_sample_kernel(lp_ref, g_ref, q_ref, ck_ref, cv_ref, nk_ref, nv_ref, o_ref, qq_s, bias_s, m_s, l_s, acc_s,
                        *, lam_init):
    j = pl.program_id(1)
    L = q_ref.shape[1]
    nt = (((1,), (1,)), ((), ()))

    @pl.when(j == 0)
    def _():
        lane = lax.broadcasted_iota(jnp.int32, (L, DVA), 1)
        for mp in range(2):
            keep = (lane < DHA) if mp == 0 else (lane >= DHA)
            for h in range(HA):
                qh = q_ref[0, :, h * DVA:(h + 1) * DVA]
                qq_s[(mp * HA + h) * L:(mp * HA + h + 1) * L, :] = jnp.where(keep, qh, jnp.zeros_like(qh))
        row_head = (lax.broadcasted_iota(jnp.int32, bias_s.shape, 0) // L) % HA
        key_head = lax.broadcasted_iota(jnp.int32, bias_s.shape, 1) % HA
        bias_s[...] = jnp.where(row_head == key_head, 0.0, NEG_BIG)
        m_s[...] = jnp.full(m_s.shape, NEG_BIG, F32)
        l_s[...] = jnp.zeros_like(l_s)
        acc_s[...] = jnp.zeros_like(acc_s)

    def update(kf, vf):
        nk = kf.shape[0]
        s = lax.dot_general(qq_s[...], kf.astype(BF16), nt, preferred_element_type=F32) + bias_s[:, :nk]
        m = m_s[...]
        n = jnp.maximum(m, jnp.max(s, axis=1, keepdims=True))
        p = jnp.exp2(s - n)
        a = jnp.exp2(m - n)
        l_s[...] = a * l_s[...] + jnp.sum(p, axis=1, keepdims=True)
        acc_s[...] = a * acc_s[...] + jnp.dot(p.astype(BF16), vf.astype(BF16), preferred_element_type=F32)
        m_s[...] = n

    update(ck_ref[0, 0], cv_ref[0, 0])

    @pl.when(j == pl.num_programs(1) - 1)
    def _():
        update(nk_ref[0], nv_ref[0])
        lam = _lambda(lp_ref, lam_init)
        o = acc_s[...] / l_s[...]
        for h in range(HA):
            oh = o[h * L:(h + 1) * L] - lam * o[(HA + h) * L:(HA + h + 1) * L]
            ms = jnp.mean(oh * oh, axis=1, keepdims=True)
            o_ref[0, :, h * DVA:(h + 1) * DVA] = oh * lax.rsqrt(ms + LN_EPS) * g_ref[...] * (1.0 - lam_init)


def _attn_sample(lp, g_row, q_bf, ck_all, cv_all, nk, nv, lam_init, layer, tkv):
    B, L, _ = q_bf.shape
    rows = ck_all.shape[2]
    kb = tkv * HA
    per_b = lambda b, j: (b, 0, 0)
    blk = pl.BlockSpec((1, 1, kb, DVA), lambda b, j: (layer, b, j, 0))
    new = pl.BlockSpec((1, L * HA, DVA), per_b)
    nq = 2 * HA * L
    return pl.pallas_call(
        functools.partial(_attn_sample_kernel, lam_init=lam_init),
        grid=(B, rows // kb),
        in_specs=[_resident(lp.shape), _resident(g_row.shape), pl.BlockSpec((1, L, HA * DVA), per_b),
                  blk, blk, new, new],
        out_specs=pl.BlockSpec((1, L, HA * DVA), per_b),
        out_shape=jax.ShapeDtypeStruct((B, L, HA * DVA), F32),
        scratch_shapes=[pltpu.VMEM((nq, DVA), BF16), pltpu.VMEM((nq, kb), F32),
                        pltpu.VMEM((nq, 1), F32), pltpu.VMEM((nq, 1), F32), pltpu.VMEM((nq, DVA), F32)],
        compiler_params=_cparams(("parallel", "arbitrary")),
        name="diff_attn_sample",
    )(lp, g_row, q_bf, ck_all, cv_all, nk, nv)


def _mlstm_kernel(bif_ref, q_ref, k_ref, v_ref, om_ref, gc_ref, gr_ref, wq_ref, wk_ref, bq_ref, bk_ref,
                  cq_ref, ck_ref, c0_ref, n0_ref, m0_ref, mg_ref,
                  ga_ref, oa_ref, gb_ref, x_ref, wo_ref, lng_ref, lnb_ref,
                  xf_ref, xb_ref, c_ref, n_ref, m_ref, qpad, kpad, hm_s, *, lc):
    c = pl.program_id(1)
    npre = CONV_M - 1
    off = SUBLANES - npre

    @pl.when(c == 0)
    def _():
        c_ref[...] = c0_ref[...]
        n_ref[...] = n0_ref[...]
        m_ref[...] = m0_ref[...]
        qpad[0:SUBLANES, :] = jnp.zeros((SUBLANES, HM * DKM), F32)
        kpad[0:SUBLANES, :] = jnp.zeros((SUBLANES, HM * DKM), F32)
        qpad[off:SUBLANES, :] = cq_ref[0]
        kpad[off:SUBLANES, :] = ck_ref[0]

    qpad[SUBLANES:SUBLANES + lc, :] = q_ref[0]
    kpad[SUBLANES:SUBLANES + lc, :] = k_ref[0]

    def conv_silu(pad, w_ref, b_ref, cs):
        xa = pad[:, cs]
        y = b_ref[:, cs]
        for j in range(CONV_M):
            xs = pltpu.roll(xa, lc + SUBLANES - (off + j), 0)
            y = y + w_ref[j:j + 1, cs] * xs[0:lc, :]
        return y * _sigmoid(y)

    t_i = lax.broadcasted_iota(jnp.int32, (lc, lc), 0)
    s_i = lax.broadcasted_iota(jnp.int32, (lc, lc), 1)
    causal = s_i <= t_i
    zeros = jnp.zeros((lc, lc), F32)

    for h in range(HM):
        cs = slice(h * DKM, (h + 1) * DKM)
        q = conv_silu(qpad, wq_ref, bq_ref, cs)
        k = conv_silu(kpad, wk_ref, bk_ref, cs) * (DKM ** -0.5)

        bi = bif_ref[h]
        bf = bif_ref[HM + h]
        gc = gc_ref[0, h]
        gr = gr_ref[0, h]
        ig_c = gc[:, 0:1] + bi
        lf_c = _log_sigmoid(gc[:, 1:2] + bf)
        ig_r = gr[0:1, :] + bi
        lf_r = _log_sigmoid(gr[1:2, :] + bf)
        bcum_c = jnp.sum(jnp.where(causal, lf_r, zeros), axis=1, keepdims=True)
        bcum_r = jnp.sum(jnp.where(t_i <= s_i, lf_c, zeros), axis=0, keepdims=True)

        m_prev = m_ref[0, h][:, 0:1]
        d = jnp.where(causal, bcum_c - bcum_r + ig_r, NEG_BIG)
        inter = bcum_c + m_prev
        m_t = jnp.maximum(inter, jnp.max(d, axis=1, keepdims=True))
        w = jnp.exp(d - m_t)
        g = jnp.exp(inter - m_t)

        qb = q.astype(BF16)
        kb = k.astype(BF16)
        vb = v_ref[0, :, cs]
        cmat = c_ref[0, h]
        nrow = n_ref[0, h]
        s = lax.dot_general(qb, kb, (((1,), (1,)), ((), ())), preferred_element_type=F32) * w
        num = (jnp.dot(s.astype(BF16), vb, preferred_element_type=F32)
               + g * jnp.dot(qb, cmat.astype(BF16), preferred_element_type=F32))
        den = jnp.sum(s, axis=1, keepdims=True) + g * jnp.sum(q * nrow, axis=1, keepdims=True)
        den = jnp.maximum(jnp.abs(den), jnp.exp(-m_t))
        hh = num / den

        m_end = m_t[lc - 1:lc, :]
        b_last = bcum_c[lc - 1:lc, :]
        w_end = jnp.exp(b_last - bcum_c + ig_c - m_end)
        g_end = jnp.exp(b_last + m_prev - m_end)
        kw = k * w_end
        c_ref[0, h] = g_end * cmat + lax.dot_general(kw.astype(BF16), vb, (((0,), (0,)), ((), ())),
                                                     preferred_element_type=F32)
        n_ref[0, h] = g_end * nrow + jnp.sum(kw, axis=0, keepdims=True)
        m_ref[0, h] = jnp.broadcast_to(m_end, (1, LANES))

        mu = jnp.mean(hh, axis=1, keepdims=True)
        hc = hh - mu
        var = jnp.mean(hc * hc, axis=1, keepdims=True)
        hm_s[:, cs] = hc * lax.rsqrt(var + LN_EPS) * mg_ref[:, cs] * _sigmoid(om_ref[0, :, cs])

    qpad[0:SUBLANES, :] = qpad[lc:lc + SUBLANES, :]
    kpad[0:SUBLANES, :] = kpad[lc:lc + SUBLANES, :]

    y = _sigmoid(ga_ref[0]) * oa_ref[0] + _sigmoid(gb_ref[0]) * hm_s[...]
    z = jnp.dot(y.astype(BF16), wo_ref[...], preferred_element_type=F32)
    out = _layernorm_rows(ALPHA * x_ref[0] + z, lng_ref[...], lnb_ref[...])
    xf_ref[0] = out
    xb_ref[0] = out.astype(BF16)


def _mlstm_merge(b_if, qkm, vm_bf, om, gcol, grow, conv_w, conv_b, conv_state, c0, n0, m0, mh_g,
                 ga, oa, gb, x, w_out, ln_g, ln_b, lc):
    B, S, D = x.shape
    nc = S // lc
    npre = CONV_M - 1
    wq, wv = HM * DKM, HM * DVM
    rowq = lambda b, c: (b, c, 0)
    rowk = lambda b, c: (b, c, 1)
    st = lambda b, c: (b, 0, 0, 0)
    blk = pl.BlockSpec((1, lc, D), rowq)
    return pl.pallas_call(
        functools.partial(_mlstm_kernel, lc=lc),
        grid=(B, nc),
        in_specs=[pl.BlockSpec(memory_space=pltpu.SMEM),
                  pl.BlockSpec((1, lc, wq), rowq), pl.BlockSpec((1, lc, wq), rowk),
                  pl.BlockSpec((1, lc, wv), rowq), pl.BlockSpec((1, lc, wv), rowq),
                  pl.BlockSpec((1, HM, lc, 2), lambda b, c: (b, 0, c, 0)),
                  pl.BlockSpec((1, HM, 2, lc), lambda b, c: (b, 0, 0, c)),
                  pl.BlockSpec((CONV_M, wq), lambda b, c: (0, 0)), pl.BlockSpec((CONV_M, wq), lambda b, c: (0, 1)),
                  pl.BlockSpec((1, wq), lambda b, c: (0, 0)), pl.BlockSpec((1, wq), lambda b, c: (0, 1)),
                  pl.BlockSpec((1, npre, wq), lambda b, c: (b, 0, 0)),
                  pl.BlockSpec((1, npre, wq), lambda b, c: (b, 0, 1)),
                  pl.BlockSpec((1, HM, DKM, DVM), st), pl.BlockSpec((1, HM, 1, DKM), st),
                  pl.BlockSpec((1, HM, 1, LANES), st), _resident(mh_g.shape),
                  blk, blk, blk, blk, _resident(w_out.shape), _resident(ln_g.shape), _resident(ln_b.shape)],
        out_specs=[blk, blk,
                   pl.BlockSpec((1, HM, DKM, DVM), st), pl.BlockSpec((1, HM, 1, DKM), st),
                   pl.BlockSpec((1, HM, 1, LANES), st)],
        out_shape=[jax.ShapeDtypeStruct((B, S, D), F32), jax.ShapeDtypeStruct((B, S, D), BF16),
                   jax.ShapeDtypeStruct((B, HM, DKM, DVM), F32),
                   jax.ShapeDtypeStruct((B, HM, 1, DKM), F32),
                   jax.ShapeDtypeStruct((B, HM, 1, LANES), F32)],
        scratch_shapes=[pltpu.VMEM((lc + SUBLANES, wq), F32), pltpu.VMEM((lc + SUBLANES, wq), F32),
                        pltpu.VMEM((lc, wv), F32)],
        compiler_params=_cparams(("parallel", "arbitrary")),
        name="mlstm_merge_ln",
    )(b_if, qkm, qkm, vm_bf, om, gcol, grow, conv_w, conv_w, conv_b, conv_b, conv_state, conv_state,
      c0, n0, m0, mh_g, ga, oa, gb, x, w_out, ln_g, ln_b)


def _ffn_kernel(xb_ref, xf_ref, wu_ref, cw_ref, cb_ref, buf_ref, wd_ref, g_ref, b_ref,
                of_ref, ob_ref, tail_ref, upad, hbuf, *, tm):
    npre = CONV_F - 1
    off = SUBLANES - npre

    @pl.when(pl.program_id(1) == 0)
    def _():
        upad[0:SUBLANES, :] = jnp.zeros((SUBLANES, 2 * D_FF), F32)
        upad[off:SUBLANES, :] = buf_ref[0]

    x = xb_ref[0]
    chunk = 512
    for c0 in range(0, 2 * D_FF, chunk):
        upad[SUBLANES:SUBLANES + tm, c0:c0 + chunk] = jnp.dot(
            x, wu_ref[:, c0:c0 + chunk], preferred_element_type=F32)

    def conv(c0, c1):
        y = cb_ref[:, c0:c1]
        for j in range(CONV_F):
            y = y + cw_ref[j:j + 1, c0:c1] * upad[off + j:off + j + tm, c0:c1]
        return y

    hchunk = 256
    for c0 in range(0, D_FF, hchunk):
        a = conv(c0, c0 + hchunk)
        gate = conv(D_FF + c0, D_FF + c0 + hchunk)
        act = 0.5 * a * (1.0 + lax.erf(a * (2.0 ** -0.5)))
        hbuf[:, c0:c0 + hchunk] = (act * gate).astype(BF16)

    tail = upad[tm:tm + SUBLANES, :]
    tail_ref[0] = tail
    upad[0:SUBLANES, :] = tail

    z = jnp.dot(hbuf[...], wd_ref[...], preferred_element_type=F32)
    out = _layernorm_rows(ALPHA * xf_ref[0] + z, g_ref[...], b_ref[...])
    of_ref[0] = out
    ob_ref[0] = out.astype(BF16)


def _ffn(x_bf, x_f32, w_up, conv_w, conv_b, buf, w_down, g, b, tm):
    B, S, D = x_f32.shape
    npre = CONV_F - 1
    row = lambda bb, s: (bb, s, 0)
    blk = pl.BlockSpec((1, tm, D), row)
    per_b = lambda bb, s: (bb, 0, 0)
    return pl.pallas_call(
        functools.partial(_ffn_kernel, tm=tm),
        grid=(B, S // tm),
        in_specs=[blk, blk, _resident(w_up.shape), _resident(conv_w.shape), _resident(conv_b.shape),
                  pl.BlockSpec((1, npre, 2 * D_FF), per_b), _resident(w_down.shape),
                  _resident(g.shape), _resident(b.shape)],
        out_specs=[blk, blk, pl.BlockSpec((1, SUBLANES, 2 * D_FF), per_b)],
        out_shape=[jax.ShapeDtypeStruct((B, S, D), F32), jax.ShapeDtypeStruct((B, S, D), BF16),
                   jax.ShapeDtypeStruct((B, SUBLANES, 2 * D_FF), F32)],
        scratch_shapes=[pltpu.VMEM((tm + SUBLANES, 2 * D_FF), F32), pltpu.VMEM((tm, D_FF), BF16)],
        compiler_params=_cparams(("parallel", "arbitrary")),
        name="convffn_ln",
    )(x_bf, x_f32, w_up, conv_w, conv_b, buf, w_down, g, b)


def _rope_tables(pos):
    half = DHA // 2
    inv = ROPE_THETA ** (-jnp.arange(half, dtype=F32) * 2.0 / DHA)
    ang = pos.astype(F32)[:, None] * inv[None, :]
    cos = jnp.tile(jnp.cos(ang), (1, LANES // half))
    sin = jnp.tile(jnp.sin(ang), (1, LANES // half))
    first = (jnp.arange(LANES) % DHA) < half
    return cos, jnp.where(first[None, :], -sin, sin)


def _split_weights(w_in):
    offs = [0]
    for s in IN_SIZES:
        offs.append(offs[-1] + s)
    part = lambda i: w_in[:, offs[i]:offs[i + 1]].astype(BF16)
    w_qkv = w_in[:, :offs[3]].astype(BF16)
    w_gif = jnp.pad(part(6), ((0, 0), (0, LANES - W_GIF)))
    return w_qkv, [part(3), part(4), part(5), part(7), part(8), w_gif]


MIX_DTYPES = [F32, BF16, F32, F32, F32, F32]


def _mixer_tail(x_f32, qkm, vm_bf, om, ga, gb, gif, oa, conv_m_buf, m_state, ffn_buf, p, cfg):
    B, S, _ = x_f32.shape
    g4 = gif[..., :W_GIF].reshape(B, S, 2, HM)
    gcol = g4.transpose(0, 3, 1, 2)
    grow = g4.transpose(0, 3, 2, 1)
    if m_state is None:
        c0 = jnp.zeros((B, HM, DKM, DVM), F32)
        n0 = jnp.zeros((B, HM, 1, DKM), F32)
        m0 = jnp.zeros((B, HM, 1, LANES), F32)
    else:
        c0 = m_state[0].astype(F32)
        n0 = m_state[1].astype(F32).reshape(B, HM, 1, DKM)
        m0 = jnp.broadcast_to(m_state[2].astype(F32)[:, :, None, None], (B, HM, 1, LANES))
    row = lambda a: a.reshape(1, -1)
    x1_f32, x1_bf, c_new, n_new, m_new = _mlstm_merge(
        p['b_if'].astype(F32), qkm, vm_bf, om, gcol, grow, p['conv_m_w'], row(p['conv_m_b']), conv_m_buf,
        c0, n0, m0, row(p['mh_g']), ga, oa, gb, x_f32, p['w_out'].astype(BF16), row(p['ln1_g']), row(p['ln1_b']),
        cfg['lc'])
    new_conv_m = jnp.concatenate([conv_m_buf, qkm[:, S - min(S, CONV_M - 1):]], axis=1)[:, -(CONV_M - 1):]

    x2_f32, x2_bf, tail = _ffn(x1_bf, x1_f32, p['w_up'].astype(BF16), p['ffn_conv_w'], row(p['ffn_conv_b']),
                               ffn_buf, p['w_down'].astype(BF16), row(p['ln2_g']), row(p['ln2_b']),
                               cfg['tm_ffn'])
    new_ffn = tail[:, SUBLANES - (CONV_F - 1):]
    return (x2_f32, x2_bf), (new_conv_m, c_new, n_new.reshape(B, HM, DKM), m_new[:, :, 0, 0], new_ffn)


def _layer_prompt(x, layer, kv_all, p, lam_init, cfg):
    x_f32, x_mm = x
    B, S, _ = x_f32.shape
    w_qkv, w_mix = _split_weights(p['w_in'])
    cos, sin = _rope_tables(jnp.arange(S))
    qt, k_all, k_bf, v_all, vt = _qkv_proj_prompt(x_mm, w_qkv, cos, sin, cfg['tk'], layer, kv_all)
    qkm, vm_bf, om, ga, gb, gif = _proj(x_mm, w_mix, MIX_DTYPES, cfg['tm_proj'])
    oa = _attn_prompt(p['lam'].astype(F32), p['subln_g'].reshape(DVA, 1), qt, k_bf, vt, cfg['tq'], cfg['chunk'],
                      lam_init)
    conv_m0 = jnp.zeros((B, CONV_M - 1, W_QKM), F32)
    ffn0 = jnp.zeros((B, CONV_F - 1, 2 * D_FF), F32)
    x_new, st = _mixer_tail(x_f32, qkm, vm_bf, om, ga, gb, gif, oa, conv_m0, None, ffn0, p, cfg)
    return x_new, (k_all, v_all), st


def _layer_sample(x, layer, caches, conv_m_buf, m_state, ffn_buf, p, lam_init, cfg):
    x_f32, x_mm = x
    B, S, _ = x_f32.shape
    w_qkv, w_mix = _split_weights(p['w_in'])
    cos, sin = _rope_tables(PAST_LEN + jnp.arange(S))
    cos, sin = jnp.tile(cos, (B, 1)), jnp.tile(sin, (B, 1))
    xin = x_mm.reshape(1, B * S, D_MODEL)
    unflat = lambda a: a.reshape(B, S, a.shape[-1])
    q_bf, k_f32, v_f32 = map(unflat, _qkv_proj_sample(xin, w_qkv, cos, sin))
    qkm, vm_bf, om, ga, gb, gif = map(unflat, _proj(xin, w_mix, MIX_DTYPES, B * S))
    oa = _attn_sample(p['lam'].astype(F32), p['subln_g'].reshape(1, DVA), q_bf, caches[0], caches[1],
                      k_f32.reshape(B, S * HA, 2 * DHA), v_f32.reshape(B, S * HA, DVA), lam_init, layer, cfg['tkv'])
    x_new, st = _mixer_tail(x_f32, qkm, vm_bf, om, ga, gb, gif, oa, conv_m_buf, m_state, ffn_buf, p, cfg)
    return x_new, (k_f32.reshape(B, S, HA, 2 * DHA), v_f32.reshape(B, S, HA, DVA)) + st


def _prompt_cfg(S):
    return dict(tm_proj=min(S, 512), tq=min(S, 2048), tk=min(S, 512), chunk=64, lc=min(S, 256),
                tm_ffn=min(S, 512))


def _sample_cfg(S, P):
    return dict(lc=S, tm_ffn=S, tkv=min(P, 512))


def kernel(x_prompt, x_sample, cache_k, cache_v, state_mlstm_conv, state_mlstm_C, state_mlstm_n,
           state_mlstm_m, state_ffn_conv, w_in, b_if, mlstm_conv_w, mlstm_conv_b, diff_lambda,
           diff_subln_g, mlstm_norm_g, w_out, ln1_g, ln1_b, w_up, ffn_conv_w, ffn_conv_b, w_down,
           ln2_g, ln2_b):
    bp, sp, _ = x_prompt.shape
    bs, ss, _ = x_sample.shape
    past = cache_k.shape[2]
    xp = (x_prompt, x_prompt)
    xs = (x_sample, x_sample)
    caches = (cache_k.reshape(DEPTH, bs, past * HA, 2 * DHA), cache_v.reshape(DEPTH, bs, past * HA, DVA))
    kv_all = None
    st_p, st_s = [], []
    for l in range(DEPTH):
        p = {'w_in': w_in[l], 'b_if': b_if[l], 'conv_m_w': mlstm_conv_w[l], 'conv_m_b': mlstm_conv_b[l],
             'lam': diff_lambda[l], 'subln_g': diff_subln_g[l], 'mh_g': mlstm_norm_g[l], 'w_out': w_out[l],
             'ln1_g': ln1_g[l], 'ln1_b': ln1_b[l], 'w_up': w_up[l], 'ffn_conv_w': ffn_conv_w[l],
             'ffn_conv_b': ffn_conv_b[l], 'w_down': w_down[l], 'ln2_g': ln2_g[l], 'ln2_b': ln2_b[l]}
        lam_init = 0.8 - 0.6 * math.exp(-0.3 * l)
        xp, kv_all, sp_l = _layer_prompt(xp, l, kv_all, p, lam_init, _prompt_cfg(sp))
        xs, ss_l = _layer_sample(xs, l, caches, state_mlstm_conv[l],
                                 (state_mlstm_C[l], state_mlstm_n[l], state_mlstm_m[l]), state_ffn_conv[l],
                                 p, lam_init, _sample_cfg(ss, past))
        st_p.append(sp_l)
        st_s.append(ss_l)
    stk = lambda lst, i: jnp.stack([s[i] for s in lst])
    k_prompt = kv_all[0].reshape(DEPTH, bp, sp, HA, 2 * DHA)
    v_prompt = kv_all[1].reshape(DEPTH, bp, sp, HA, DVA)
    return ((xp[0], xs[0], k_prompt, v_prompt) + tuple(stk(st_p, i) for i in range(5))
            + tuple(stk(st_s, i) for i in range(7)))
```

```python
import functools
import math

import jax
import jax.numpy as jnp
from jax import lax
from jax.experimental import pallas as pl
from jax.experimental.pallas import tpu as pltpu

F32 = jnp.float32
BF16 = jnp.bfloat16

D_MODEL = 1024
DEPTH = 2
PAST_LEN = 4096
HA = 8
DHA = 64
DVA = 2 * DHA
HM = 4
DKM = 256
DVM = 256
CONV_M = 4
D_FF = 2816
CONV_F = 3
ROPE_THETA = 10000.0
LN_EPS = 1e-5
ALPHA = (2 * DEPTH) ** 0.25

W_QA = HA * 2 * DHA
W_KA = HA * 2 * DHA
W_VA = HA * DVA
W_QKM = 2 * HM * DKM
W_VM = HM * DVM
W_OM = HM * DVM
W_GIF = 2 * HM
W_GA = HA * DVA
W_GB = HM * DVM
IN_SIZES = (W_QA, W_KA, W_VA, W_QKM, W_VM, W_OM, W_GIF, W_GA, W_GB)

LANES = 128
SUBLANES = 8
NEG_BIG = -1e30
LOG2E = math.log2(math.e)
Q_SCALE = DHA ** -0.5 * LOG2E
EXP_SPAN = 64.0
NORM_SLACK = 1.05
VMEM_LIMIT = 52 * 1024 * 1024


def _cparams(sem):
    return pltpu.CompilerParams(dimension_semantics=sem, vmem_limit_bytes=VMEM_LIMIT)


def _resident(shape):
    nd = len(shape)
    return pl.BlockSpec(shape, lambda *_: (0,) * nd, pipeline_mode=pl.Buffered(1))


def _layernorm_rows(r, g, b):
    mu = jnp.mean(r, axis=-1, keepdims=True)
    d = r - mu
    var = jnp.mean(d * d, axis=-1, keepdims=True)
    return d * lax.rsqrt(var + LN_EPS) * g + b


def _sigmoid(x):
    return 0.5 * jnp.tanh(0.5 * x) + 0.5


def _log_sigmoid(x):
    return jnp.minimum(x, 0.0) - jnp.log1p(jnp.exp(-jnp.abs(x)))


def _rope_fn(cos_ref, sin_ref):
    cos = cos_ref[...]
    sin = sin_ref[...]
    lane = lax.broadcasted_iota(jnp.int32, cos.shape, 1)
    first = (lane % DHA) < (DHA // 2)

    def rope(z):
        rot = jnp.where(first, pltpu.roll(z, LANES - DHA // 2, 1), pltpu.roll(z, DHA // 2, 1))
        return z * cos + rot * sin

    return rope


def _qkv_dots(x, w_ref, c0, chunk):
    zq = jnp.dot(x, w_ref[:, c0:c0 + chunk], preferred_element_type=F32)
    zk = jnp.dot(x, w_ref[:, W_QA + c0:W_QA + c0 + chunk], preferred_element_type=F32)
    zv = jnp.dot(x, w_ref[:, W_QA + W_KA + c0:W_QA + W_KA + c0 + chunk], preferred_element_type=F32)
    return zq, zk, zv


def _qkv_prompt_kernel(x_ref, w_ref, cos_ref, sin_ref, *refs):
    qt_ref, kf_ref, kb_ref, vf_ref, vt_ref = refs[-5:]
    for slot in range(1, kf_ref.shape[0]):
        kf_ref[slot] = jnp.zeros(kf_ref.shape[1:], F32)
        vf_ref[slot] = jnp.zeros(vf_ref.shape[1:], F32)
    x = x_ref[0].astype(BF16)
    rope = _rope_fn(cos_ref, sin_ref)
    chunk = 512
    for c0 in range(0, W_QA, chunk):
        zq, zk, zv = _qkv_dots(x, w_ref, c0, chunk)
        for g0 in range(0, chunk, LANES):
            h = (c0 + g0) // LANES
            sl = slice(c0 + g0, c0 + g0 + LANES)
            q = rope(zq[:, g0:g0 + LANES]) * Q_SCALE
            k = rope(zk[:, g0:g0 + LANES])
            v = zv[:, g0:g0 + LANES]
            qt_ref[0, h, 0] = q.T.astype(BF16)
            rows = pl.ds(h, x.shape[0], stride=HA)
            kf_ref[0, 0, rows, :] = k
            kb_ref[0, :, sl] = k.astype(BF16)
            vf_ref[0, 0, rows, :] = v
            vt_ref[0, h, 0] = v.T.astype(BF16)


def _qkv_proj_prompt(x, w_qkv, cos, sin, tm, layer, kv_all):
    B, S, D = x.shape
    ns = S // tm
    row = lambda b, s: (b, s, 0)
    tab = lambda b, s: (s, 0)
    nslot = DEPTH if kv_all is None else 1
    slot = lambda b, s: (0 if kv_all is None else layer, b, s, 0)
    tr = lambda b, s: (b, 0, s, 0, 0)
    in_specs = [pl.BlockSpec((1, tm, D), row), _resident(w_qkv.shape),
                pl.BlockSpec((tm, LANES), tab), pl.BlockSpec((tm, LANES), tab)]
    args = [x, w_qkv, cos, sin]
    aliases = {}
    if kv_all is not None:
        in_specs += [pl.BlockSpec(memory_space=pl.ANY)] * 2
        args += list(kv_all)
        aliases = {4: 1, 5: 3}
    t_shape = jax.ShapeDtypeStruct((B, HA, ns, DVA, tm), BF16)
    all_shape = jax.ShapeDtypeStruct((DEPTH, B, S * HA, DVA), F32)
    return pl.pallas_call(
        _qkv_prompt_kernel,
        grid=(B, ns),
        in_specs=in_specs,
        out_specs=[pl.BlockSpec((1, HA, 1, DVA, tm), tr), pl.BlockSpec((nslot, 1, tm * HA, DVA), slot),
                   pl.BlockSpec((1, tm, W_KA), row), pl.BlockSpec((nslot, 1, tm * HA, DVA), slot),
                   pl.BlockSpec((1, HA, 1, DVA, tm), tr)],
        out_shape=[t_shape, all_shape, jax.ShapeDtypeStruct((B, S, W_KA), BF16), all_shape, t_shape],
        input_output_aliases=aliases,
        compiler_params=_cparams(("parallel", "parallel")),
        name="qkv_rope_prompt",
    )(*args)


def _qkv_sample_kernel(x_ref, w_ref, cos_ref, sin_ref, q_ref, kf_ref, vf_ref):
    x = x_ref[0].astype(BF16)
    rope = _rope_fn(cos_ref, sin_ref)
    chunk = 512
    for c0 in range(0, W_QA, chunk):
        zq, zk, zv = _qkv_dots(x, w_ref, c0, chunk)
        for g0 in range(0, chunk, LANES):
            sl = slice(c0 + g0, c0 + g0 + LANES)
            q_ref[0, :, sl] = (rope(zq[:, g0:g0 + LANES]) * Q_SCALE).astype(BF16)
            kf_ref[0, :, sl] = rope(zk[:, g0:g0 + LANES])
        vf_ref[0, :, c0:c0 + chunk] = zv


def _qkv_proj_sample(x, w_qkv, cos, sin):
    B, M, D = x.shape
    out = lambda dt: jax.ShapeDtypeStruct((B, M, W_QA), dt)
    return pl.pallas_call(
        _qkv_sample_kernel,
        grid=(B,),
        in_specs=[pl.BlockSpec((1, M, D), lambda b: (b, 0, 0)), _resident(w_qkv.shape),
                  _resident(cos.shape), _resident(sin.shape)],
        out_specs=[pl.BlockSpec((1, M, W_QA), lambda b: (b, 0, 0))] * 3,
        out_shape=[out(BF16), out(F32), out(F32)],
        compiler_params=_cparams(("parallel",)),
        name="qkv_rope_sample",
    )(x, w_qkv, cos, sin)


def _proj_kernel(x_ref, *refs, n_out):
    x = x_ref[0].astype(BF16)
    chunk = 512
    for w_ref, o_ref in zip(refs[:n_out], refs[n_out:]):
        n = w_ref.shape[1]
        for c0 in range(0, n, chunk):
            c1 = min(c0 + chunk, n)
            o_ref[0, :, c0:c1] = jnp.dot(x, w_ref[:, c0:c1], preferred_element_type=F32).astype(o_ref.dtype)


def _proj(x, weights, dtypes, tm):
    B, S, D = x.shape
    row = lambda b, s: (b, s, 0)
    return pl.pallas_call(
        functools.partial(_proj_kernel, n_out=len(weights)),
        grid=(B, S // tm),
        in_specs=[pl.BlockSpec((1, tm, D), row)] + [_resident(w.shape) for w in weights],
        out_specs=[pl.BlockSpec((1, tm, w.shape[1]), row) for w in weights],
        out_shape=[jax.ShapeDtypeStruct((B, S, w.shape[1]), dt) for w, dt in zip(weights, dtypes)],
        compiler_params=_cparams(("parallel", "parallel")),
        name="mix_proj",
    )(x, *weights)


def _lambda(lp_ref, lam_init):
    lp = lp_ref[...]
    a = jnp.sum(lp[0:1] * lp[1:2], axis=1, keepdims=True)
    b = jnp.sum(lp[2:3] * lp[3:4], axis=1, keepdims=True)
    return jnp.exp(a) - jnp.exp(b) + lam_init


def _attn_prompt_kernel(lp_ref, g_ref, qt_ref, k_ref, vt_ref, o_ref, acc_ref, knorm_ref,
                        *, tq, tk, chunk, lam_init):
    qi = pl.program_id(2)
    r = tq // tk
    qt = jnp.concatenate([qt_ref[0, 0, t] for t in range(r)], axis=1)
    row = lax.broadcasted_iota(jnp.int32, qt.shape, 0)
    zero = jnp.zeros_like(qt)
    qq = jnp.concatenate([jnp.where(row < DHA, qt, zero), jnp.where(row >= DHA, qt, zero)], axis=1)
    acc_ref[...] = jnp.zeros_like(acc_ref)

    @pl.when(qi == 0)
    def _():
        kf = k_ref[0].astype(F32)
        dim = lax.broadcasted_iota(jnp.int32, (DVA, LANES), 0)
        col = lax.broadcasted_iota(jnp.int32, (DVA, LANES), 1)
        sel = jnp.where((dim // DHA) == col, 1.0, 0.0).astype(BF16)
        ksq = jnp.dot((kf * kf).astype(BF16), sel, preferred_element_type=F32)
        knorm_ref[...] = jnp.max(ksq, axis=0, keepdims=True)

    def both(x, c0):
        return x if c0 == 0 else jnp.concatenate([x[:, c0:tq], x[:, tq + c0:]], axis=1)

    def merge(old, new, c0):
        if c0 == 0:
            return new
        w = tq - c0
        return jnp.concatenate([old[:, :c0], new[:, :w], old[:, tq:tq + c0], new[:, w:]], axis=1)

    def scores(j, c0, mask):
        k = k_ref[0, pl.ds(pl.multiple_of(j * tk, tk), tk), :]
        s = jnp.dot(k, both(qq, c0), preferred_element_type=F32)
        if mask is not None:
            mk = mask[:, c0:]
            s = jnp.where(jnp.concatenate([mk, mk], axis=1), s, NEG_BIG)
        return s

    def accumulate(j, c0, p, scale):
        vt = vt_ref[0, 0, j]
        w = tq - c0
        pb = p.astype(BF16)
        for mp in range(2):
            pv = jnp.dot(vt, pb[:, mp * w:(mp + 1) * w], preferred_element_type=F32)
            old = acc_ref[mp, :, c0:]
            acc_ref[mp, :, c0:] = (old if scale is None else scale[:, mp * w:(mp + 1) * w] * old) + pv

    def running_max_step(j, carry, c0=0, mask=None):
        m, l = carry
        s = scores(j, c0, mask)
        mo = both(m, c0)
        n = jnp.maximum(mo, jnp.max(s, axis=0, keepdims=True))
        p = jnp.exp2(s - n)
        a = jnp.exp2(mo - n)
        ln = a * both(l, c0) + jnp.sum(p, axis=0, keepdims=True)
        accumulate(j, c0, p, a)
        return merge(m, n, c0), merge(l, ln, c0)

    def fixed_ref_step(j, carry, c0=0, mask=None):
        m, l = carry
        p = jnp.exp2(scores(j, c0, mask) - both(m, c0))
        accumulate(j, c0, p, None)
        return m, merge(l, both(l, c0) + jnp.sum(p, axis=0, keepdims=True), c0)

    nfull = qi * r

    def diag_mask(d):
        kr = (lax.broadcasted_iota(jnp.int32, (tk, tq), 0) + d * tk) // chunk
        qc = lax.broadcasted_iota(jnp.int32, (tk, tq), 1) // chunk
        return kr <= qc

    def sweep(step):
        def run(carry):
            for d in range(1, r):
                carry = step(nfull + d, carry, d * tk, diag_mask(d))
            return lax.fori_loop(0, nfull, step, carry)
        return run

    init = (jnp.full((1, 2 * tq), NEG_BIG, F32), jnp.zeros((1, 2 * tq), F32))
    carry = running_max_step(nfull, init, 0, diag_mask(0))

    qf = qt.astype(F32)
    qsq = qf * qf
    q1 = jnp.max(jnp.sum(qsq[:DHA], axis=0, keepdims=True), axis=1, keepdims=True)
    q2 = jnp.max(jnp.sum(qsq[DHA:], axis=0, keepdims=True), axis=1, keepdims=True)
    kn = knorm_ref[...]
    worst = jnp.maximum(q1 * kn[:, 0:1], q2 * kn[:, 1:2])
    safe = jnp.max(worst) * NORM_SLACK <= (0.5 * EXP_SPAN) ** 2
    _, l = lax.cond(safe, sweep(fixed_ref_step), sweep(running_max_step), carry)

    lam = _lambda(lp_ref, lam_init)
    o = acc_ref[0] / l[:, :tq] - lam * (acc_ref[1] / l[:, tq:])
    ms = jnp.mean(o * o, axis=0, keepdims=True)
    o = o * lax.rsqrt(ms + LN_EPS) * g_ref[...] * (1.0 - lam_init)
    o_ref[0] = o.T


def _attn_prompt(lp, g_col, qt, k_bf, vt, tq, chunk, lam_init):
    B, H, nt, _, tk = vt.shape
    S = k_bf.shape[1]
    r = tq // tk
    return pl.pallas_call(
        functools.partial(_attn_prompt_kernel, tq=tq, tk=tk, chunk=chunk, lam_init=lam_init),
        grid=(B, H, S // tq),
        in_specs=[_resident(lp.shape), _resident(g_col.shape),
                  pl.BlockSpec((1, 1, r, DVA, tk), lambda b, h, i: (b, h, i, 0, 0)),
                  pl.BlockSpec((1, S, DVA), lambda b, h, i: (b, 0, h)),
                  pl.BlockSpec((1, 1, nt, DVA, tk), lambda b, h, i: (b, h, 0, 0, 0))],
        out_specs=pl.BlockSpec((1, tq, DVA), lambda b, h, i: (b, i, h)),
        out_shape=jax.ShapeDtypeStruct((B, S, H * DVA), F32),
        scratch_shapes=[pltpu.VMEM((2, DVA, tq), F32), pltpu.VMEM((1, LANES), F32)],
        compiler_params=_cparams(("parallel", "parallel", "arbitrary")),
        name="diff_attn_prompt",
    )(lp, g_col, qt, k_bf, vt)


def _attn_sample_kernel(lp_ref, g_ref, q_ref, ck_ref, cv_ref, nk_ref, nv_ref, o_ref, qq_s, bias_s, m_s, l_s, acc_s,
                        *, lam_init):
    j = pl.program_id(1)
    L = q_ref.shape[1]
    nt = (((1,), (1,)), ((), ()))

    @pl.when(j == 0)
    def _():
        lane = lax.broadcasted_iota(jnp.int32, (L, DVA), 1)
        for mp in range(2):
            keep = (lane < DHA) if mp == 0 else (lane >= DHA)
            for h in range(HA):
                qh = q_ref[0, :, h * DVA:(h + 1) * DVA]
                qq_s[(mp * HA + h) * L:(mp * HA + h + 1) * L, :] = jnp.where(keep, qh, jnp.zeros_like(qh))
        row_head = (lax.broadcasted_iota(jnp.int32, bias_s.shape, 0) // L) % HA
        key_head = lax.broadcasted_iota(jnp.int32, bias_s.shape, 1) % HA
        bias_s[...] = jnp.where(row_head == key_head, 0.0, NEG_BIG)
        m_s[...] = jnp.full(m_s.shape, NEG_BIG, F32)
        l_s[...] = jnp.zeros_like(l_s)
        acc_s[...] = jnp.zeros_like(acc_s)

    def update(kf, vf):
        nk = kf.shape[0]
        s = lax.dot_general(qq_s[...], kf.astype(BF16), nt, preferred_element_type=F32) + bias_s[:, :nk]
        m = m_s[...]
        n = jnp.maximum(m, jnp.max(s, axis=1, keepdims=True))
        p = jnp.exp2(s - n)
        a = jnp.exp2(m - n)
        l_s[...] = a * l_s[...] + jnp.sum(p, axis=1, keepdims=True)
        acc_s[...] = a * acc_s[...] + jnp.dot(p.astype(BF16), vf.astype(BF16), preferred_element_type=F32)
        m_s[...] = n

    update(ck_ref[0, 0], cv_ref[0, 0])

    @pl.when(j == pl.num_programs(1) - 1)
    def _():
        update(nk_ref[0], nv_ref[0])
        lam = _lambda(lp_ref, lam_init)
        o = acc_s[...] / l_s[...]
        for h in range(HA):
            oh = o[h * L:(h + 1) * L] - lam * o[(HA + h) * L:(HA + h + 1) * L]
            ms = jnp.mean(oh * oh, axis=1, keepdims=True)
            o_ref[0, :, h * DVA:(h + 1) * DVA] = oh * lax.rsqrt(ms + LN_EPS) * g_ref[...] * (1.0 - lam_init)


def _attn_sample(lp, g_row, q_bf, ck_all, cv_all, nk, nv, lam_init, layer, tkv):
    B, L, _ = q_bf.shape
    rows = ck_all.shape[2]
    kb = tkv * HA
    per_b = lambda b, j: (b, 0, 0)
    blk = pl.BlockSpec((1, 1, kb, DVA), lambda b, j: (layer, b, j, 0))
    new = pl.BlockSpec((1, L * HA, DVA), per_b)
    nq = 2 * HA * L
    return pl.pallas_call(
        functools.partial(_attn_sample_kernel, lam_init=lam_init),
        grid=(B, rows // kb),
        in_specs=[_resident(lp.shape), _resident(g_row.shape), pl.BlockSpec((1, L, HA * DVA), per_b),
                  blk, blk, new, new],
        out_specs=pl.BlockSpec((1, L, HA * DVA), per_b),
        out_shape=jax.ShapeDtypeStruct((B, L, HA * DVA), F32),
        scratch_shapes=[pltpu.VMEM((nq, DVA), BF16), pltpu.VMEM((nq, kb), F32),
                        pltpu.VMEM((nq, 1), F32), pltpu.VMEM((nq, 1), F32), pltpu.VMEM((nq, DVA), F32)],
        compiler_params=_cparams(("parallel", "arbitrary")),
        name="diff_attn_sample",
    )(lp, g_row, q_bf, ck_all, cv_all, nk, nv)


def _mlstm_kernel(bif_ref, q_ref, k_ref, v_ref, om_ref, gif_ref, wq_ref, wk_ref, bq_ref, bk_ref,
                  cq_ref, ck_ref, c0_ref, n0_ref, m0_ref, mg_ref,
                  ga_ref, oa_ref, gb_ref, x_ref, wo_ref, lng_ref, lnb_ref,
                  xf_ref, xb_ref, c_ref, n_ref, m_ref, qpad, kpad, hm_s, *, lc):
    c = pl.program_id(1)
    npre = CONV_M - 1
    off = SUBLANES - npre

    @pl.when(c == 0)
    def _():
        c_ref[...] = c0_ref[...]
        n_ref[...] = n0_ref[...]
        m_ref[...] = m0_ref[...]
        qpad[0:SUBLANES, :] = jnp.zeros((SUBLANES, HM * DKM), F32)
        kpad[0:SUBLANES, :] = jnp.zeros((SUBLANES, HM * DKM), F32)
        qpad[off:SUBLANES, :] = cq_ref[0]
        kpad[off:SUBLANES, :] = ck_ref[0]

    qpad[SUBLANES:SUBLANES + lc, :] = q_ref[0]
    kpad[SUBLANES:SUBLANES + lc, :] = k_ref[0]

    def conv_silu(pad, w_ref, b_ref, cs):
        xa = pad[:, cs]
        y = b_ref[:, cs]
        for j in range(CONV_M):
            xs = pltpu.roll(xa, lc + SUBLANES - (off + j), 0)
            y = y + w_ref[j:j + 1, cs] * xs[0:lc, :]
        return y * _sigmoid(y)

    t_i = lax.broadcasted_iota(jnp.int32, (lc, lc), 0)
    s_i = lax.broadcasted_iota(jnp.int32, (lc, lc), 1)
    causal = s_i <= t_i
    zeros = jnp.zeros((lc, lc), F32)

    rows_t = -(-lc // LANES) * LANES
    gpre = gif_ref[0]
    if rows_t != lc:
        gpre = jnp.concatenate([gpre, jnp.zeros((rows_t - lc, LANES), F32)], axis=0)
    grow = gpre.T[0:W_GIF, :] + bif_ref[...]
    is_forget = lax.broadcasted_iota(jnp.int32, grow.shape, 0) >= HM
    grow = jnp.where(is_forget, _log_sigmoid(grow), grow)
    gcol = jnp.concatenate([grow, jnp.zeros((LANES - W_GIF, rows_t), F32)], axis=0).T[0:lc, :]
    grow = grow[:, 0:lc]

    for h in range(HM):
        cs = slice(h * DKM, (h + 1) * DKM)
        q = conv_silu(qpad, wq_ref, bq_ref, cs)
        k = conv_silu(kpad, wk_ref, bk_ref, cs) * (DKM ** -0.5)

        ig_c = gcol[:, h:h + 1]
        lf_c = gcol[:, HM + h:HM + h + 1]
        ig_r = grow[h:h + 1, :]
        lf_r = grow[HM + h:HM + h + 1, :]
        bcum_c = jnp.sum(jnp.where(causal, lf_r, zeros), axis=1, keepdims=True)
        bcum_r = jnp.sum(jnp.where(t_i <= s_i, lf_c, zeros), axis=0, keepdims=True)

        m_prev = m_ref[0, h][:, 0:1]
        d = jnp.where(causal, bcum_c - bcum_r + ig_r, NEG_BIG)
        inter = bcum_c + m_prev
        m_t = jnp.maximum(inter, jnp.max(d, axis=1, keepdims=True))
        w = jnp.exp(d - m_t)
        g = jnp.exp(inter - m_t)

        qb = q.astype(BF16)
        kb = k.astype(BF16)
        vb = v_ref[0, :, cs]
        cmat = c_ref[0, h]
        nrow = n_ref[0, h]
        s = lax.dot_general(qb, kb, (((1,), (1,)), ((), ())), preferred_element_type=F32) * w
        num = (jnp.dot(s.astype(BF16), vb, preferred_element_type=F32)
               + g * jnp.dot(qb, cmat.astype(BF16), preferred_element_type=F32))
        den = jnp.sum(s, axis=1, keepdims=True) + g * jnp.sum(q * nrow, axis=1, keepdims=True)
        den = jnp.maximum(jnp.abs(den), jnp.exp(-m_t))
        hh = num / den

        m_end = m_t[lc - 1:lc, :]
        b_last = bcum_c[lc - 1:lc, :]
        w_end = jnp.exp(b_last - bcum_c + ig_c - m_end)
        g_end = jnp.exp(b_last + m_prev - m_end)
        kw = k * w_end
        c_ref[0, h] = g_end * cmat + lax.dot_general(kw.astype(BF16), vb, (((0,), (0,)), ((), ())),
                                                     preferred_element_type=F32)
        n_ref[0, h] = g_end * nrow + jnp.sum(kw, axis=0, keepdims=True)
        m_ref[0, h] = jnp.broadcast_to(m_end, (1, LANES))

        mu = jnp.mean(hh, axis=1, keepdims=True)
        hc = hh - mu
        var = jnp.mean(hc * hc, axis=1, keepdims=True)
        hm_s[:, cs] = hc * lax.rsqrt(var + LN_EPS) * mg_ref[:, cs] * _sigmoid(om_ref[0, :, cs])

    qpad[0:SUBLANES, :] = qpad[lc:lc + SUBLANES, :]
    kpad[0:SUBLANES, :] = kpad[lc:lc + SUBLANES, :]

    y = _sigmoid(ga_ref[0]) * oa_ref[0] + _sigmoid(gb_ref[0]) * hm_s[...]
    z = jnp.dot(y.astype(BF16), wo_ref[...], preferred_element_type=F32)
    out = _layernorm_rows(ALPHA * x_ref[0] + z, lng_ref[...], lnb_ref[...])
    xf_ref[0] = out
    xb_ref[0] = out.astype(BF16)


def _mlstm_merge(b_if, qkm, vm_bf, om, gif, conv_w, conv_b, conv_state, c0, n0, m0, mh_g,
                 ga, oa, gb, x, w_out, ln_g, ln_b, lc):
    B, S, D = x.shape
    nc = S // lc
    npre = CONV_M - 1
    wq, wv = HM * DKM, HM * DVM
    rowq = lambda b, c: (b, c, 0)
    rowk = lambda b, c: (b, c, 1)
    st = lambda b, c: (b, 0, 0, 0)
    blk = pl.BlockSpec((1, lc, D), rowq)
    return pl.pallas_call(
        functools.partial(_mlstm_kernel, lc=lc),
        grid=(B, nc),
        in_specs=[_resident(b_if.shape),
                  pl.BlockSpec((1, lc, wq), rowq), pl.BlockSpec((1, lc, wq), rowk),
                  pl.BlockSpec((1, lc, wv), rowq), pl.BlockSpec((1, lc, wv), rowq),
                  pl.BlockSpec((1, lc, LANES), rowq),
                  pl.BlockSpec((CONV_M, wq), lambda b, c: (0, 0)), pl.BlockSpec((CONV_M, wq), lambda b, c: (0, 1)),
                  pl.BlockSpec((1, wq), lambda b, c: (0, 0)), pl.BlockSpec((1, wq), lambda b, c: (0, 1)),
                  pl.BlockSpec((1, npre, wq), lambda b, c: (b, 0, 0)),
                  pl.BlockSpec((1, npre, wq), lambda b, c: (b, 0, 1)),
                  pl.BlockSpec((1, HM, DKM, DVM), st), pl.BlockSpec((1, HM, 1, DKM), st),
                  pl.BlockSpec((1, HM, 1, LANES), st), _resident(mh_g.shape),
                  blk, blk, blk, blk, _resident(w_out.shape), _resident(ln_g.shape), _resident(ln_b.shape)],
        out_specs=[blk, blk,
                   pl.BlockSpec((1, HM, DKM, DVM), st), pl.BlockSpec((1, HM, 1, DKM), st),
                   pl.BlockSpec((1, HM, 1, LANES), st)],
        out_shape=[jax.ShapeDtypeStruct((B, S, D), F32), jax.ShapeDtypeStruct((B, S, D), BF16),
                   jax.ShapeDtypeStruct((B, HM, DKM, DVM), F32),
                   jax.ShapeDtypeStruct((B, HM, 1, DKM), F32),
                   jax.ShapeDtypeStruct((B, HM, 1, LANES), F32)],
        scratch_shapes=[pltpu.VMEM((lc + SUBLANES, wq), F32), pltpu.VMEM((lc + SUBLANES, wq), F32),
                        pltpu.VMEM((lc, wv), F32)],
        compiler_params=_cparams(("parallel", "arbitrary")),
        name="mlstm_merge_ln",
    )(b_if, qkm, qkm, vm_bf, om, gif, conv_w, conv_w, conv_b, conv_b, conv_state, conv_state,
      c0, n0, m0, mh_g, ga, oa, gb, x, w_out, ln_g, ln_b)


def _ffn_kernel(xb_ref, xf_ref, wu_ref, cw_ref, cb_ref, buf_ref, wd_ref, g_ref, b_ref,
                of_ref, ob_ref, tail_ref, upad, hbuf, *, tm):
    npre = CONV_F - 1
    off = SUBLANES - npre

    @pl.when(pl.program_id(1) == 0)
    def _():
        upad[0:SUBLANES, :] = jnp.zeros((SUBLANES, 2 * D_FF), F32)
        upad[off:SUBLANES, :] = buf_ref[0]

    x = xb_ref[0]
    chunk = 512
    for c0 in range(0, 2 * D_FF, chunk):
        upad[SUBLANES:SUBLANES + tm, c0:c0 + chunk] = jnp.dot(
            x, wu_ref[:, c0:c0 + chunk], preferred_element_type=F32)

    def conv(c0, c1):
        y = cb_ref[:, c0:c1]
        for j in range(CONV_F):
            y = y + cw_ref[j:j + 1, c0:c1] * upad[off + j:off + j + tm, c0:c1]
        return y

    hchunk = 256
    for c0 in range(0, D_FF, hchunk):
        a = conv(c0, c0 + hchunk)
        gate = conv(D_FF + c0, D_FF + c0 + hchunk)
        act = 0.5 * a * (1.0 + lax.erf(a * (2.0 ** -0.5)))
        hbuf[:, c0:c0 + hchunk] = (act * gate).astype(BF16)

    tail = upad[tm:tm + SUBLANES, :]
    tail_ref[0] = tail
    upad[0:SUBLANES, :] = tail

    z = jnp.dot(hbuf[...], wd_ref[...], preferred_element_type=F32)
    out = _layernorm_rows(ALPHA * xf_ref[0] + z, g_ref[...], b_ref[...])
    of_ref[0] = out
    ob_ref[0] = out.astype(BF16)


def _ffn(x_bf, x_f32, w_up, conv_w, conv_b, buf, w_down, g, b, tm):
    B, S, D = x_f32.shape
    npre = CONV_F - 1
    row = lambda bb, s: (bb, s, 0)
    blk = pl.BlockSpec((1, tm, D), row)
    per_b = lambda bb, s: (bb, 0, 0)
    return pl.pallas_call(
        functools.partial(_ffn_kernel, tm=tm),
        grid=(B, S // tm),
        in_specs=[blk, blk, _resident(w_up.shape), _resident(conv_w.shape), _resident(conv_b.shape),
                  pl.BlockSpec((1, npre, 2 * D_FF), per_b), _resident(w_down.shape),
                  _resident(g.shape), _resident(b.shape)],
        out_specs=[blk, blk, pl.BlockSpec((1, SUBLANES, 2 * D_FF), per_b)],
        out_shape=[jax.ShapeDtypeStruct((B, S, D), F32), jax.ShapeDtypeStruct((B, S, D), BF16),
                   jax.ShapeDtypeStruct((B, SUBLANES, 2 * D_FF), F32)],
        scratch_shapes=[pltpu.VMEM((tm + SUBLANES, 2 * D_FF), F32), pltpu.VMEM((tm, D_FF), BF16)],
        compiler_params=_cparams(("parallel", "arbitrary")),
        name="convffn_ln",
    )(x_bf, x_f32, w_up, conv_w, conv_b, buf, w_down, g, b)


def _rope_tables(pos):
    half = DHA // 2
    inv = ROPE_THETA ** (-jnp.arange(half, dtype=F32) * 2.0 / DHA)
    ang = pos.astype(F32)[:, None] * inv[None, :]
    cos = jnp.tile(jnp.cos(ang), (1, LANES // half))
    sin = jnp.tile(jnp.sin(ang), (1, LANES // half))
    first = (jnp.arange(LANES) % DHA) < half
    return cos, jnp.where(first[None, :], -sin, sin)


def _split_weights(w_in):
    offs = [0]
    for s in IN_SIZES:
        offs.append(offs[-1] + s)
    part = lambda i: w_in[:, offs[i]:offs[i + 1]].astype(BF16)
    w_qkv = w_in[:, :offs[3]].astype(BF16)
    w_gif = jnp.pad(part(6), ((0, 0), (0, LANES - W_GIF)))
    return w_qkv, [part(3), part(4), part(5), part(7), part(8), w_gif]


MIX_DTYPES = [F32, BF16, F32, F32, F32, F32]


def _mixer_tail(x_f32, qkm, vm_bf, om, ga, gb, gif, oa, conv_m_buf, m_state, ffn_buf, p, cfg):
    B, S, _ = x_f32.shape
    if m_state is None:
        c0 = jnp.zeros((B, HM, DKM, DVM), F32)
        n0 = jnp.zeros((B, HM, 1, DKM), F32)
        m0 = jnp.zeros((B, HM, 1, LANES), F32)
    else:
        c0 = m_state[0].astype(F32)
        n0 = m_state[1].astype(F32).reshape(B, HM, 1, DKM)
        m0 = jnp.broadcast_to(m_state[2].astype(F32)[:, :, None, None], (B, HM, 1, LANES))
    row = lambda a: a.reshape(1, -1)
    x1_f32, x1_bf, c_new, n_new, m_new = _mlstm_merge(
        p['b_if'].astype(F32).reshape(W_GIF, 1), qkm, vm_bf, om, gif, p['conv_m_w'], row(p['conv_m_b']), conv_m_buf,
        c0, n0, m0, row(p['mh_g']), ga, oa, gb, x_f32, p['w_out'].astype(BF16), row(p['ln1_g']), row(p['ln1_b']),
        cfg['lc'])
    new_conv_m = jnp.concatenate([conv_m_buf, qkm[:, S - min(S, CONV_M - 1):]], axis=1)[:, -(CONV_M - 1):]

    x2_f32, x2_bf, tail = _ffn(x1_bf, x1_f32, p['w_up'].astype(BF16), p['ffn_conv_w'], row(p['ffn_conv_b']),
                               ffn_buf, p['w_down'].astype(BF16), row(p['ln2_g']), row(p['ln2_b']),
                               cfg['tm_ffn'])
    new_ffn = tail[:, SUBLANES - (CONV_F - 1):]
    return (x2_f32, x2_bf), (new_conv_m, c_new, n_new.reshape(B, HM, DKM), m_new[:, :, 0, 0], new_ffn)


def _layer_prompt(x, layer, kv_all, p, lam_init, cfg):
    x_f32, x_mm = x
    B, S, _ = x_f32.shape
    w_qkv, w_mix = _split_weights(p['w_in'])
    cos, sin = _rope_tables(jnp.arange(S))
    qt, k_all, k_bf, v_all, vt = _qkv_proj_prompt(x_mm, w_qkv, cos, sin, cfg['tk'], layer, kv_all)
    qkm, vm_bf, om, ga, gb, gif = _proj(x_mm, w_mix, MIX_DTYPES, cfg['tm_proj'])
    oa = _attn_prompt(p['lam'].astype(F32), p['subln_g'].reshape(DVA, 1), qt, k_bf, vt, cfg['tq'], cfg['chunk'],
                      lam_init)
    conv_m0 = jnp.zeros((B, CONV_M - 1, W_QKM), F32)
    ffn0 = jnp.zeros((B, CONV_F - 1, 2 * D_FF), F32)
    x_new, st = _mixer_tail(x_f32, qkm, vm_bf, om, ga, gb, gif, oa, conv_m0, None, ffn0, p, cfg)
    return x_new, (k_all, v_all), st


def _layer_sample(x, layer, caches, conv_m_buf, m_state, ffn_buf, p, lam_init, cfg):
    x_f32, x_mm = x
    B, S, _ = x_f32.shape
    w_qkv, w_mix = _split_weights(p['w_in'])
    cos, sin = _rope_tables(PAST_LEN + jnp.arange(S))
    cos, sin = jnp.tile(cos, (B, 1)), jnp.tile(sin, (B, 1))
    xin = x_mm.reshape(1, B * S, D_MODEL)
    unflat = lambda a: a.reshape(B, S, a.shape[-1])
    q_bf, k_f32, v_f32 = map(unflat, _qkv_proj_sample(xin, w_qkv, cos, sin))
    qkm, vm_bf, om, ga, gb, gif = map(unflat, _proj(xin, w_mix, MIX_DTYPES, B * S))
    oa = _attn_sample(p['lam'].astype(F32), p['subln_g'].reshape(1, DVA), q_bf, caches[0], caches[1],
                      k_f32.reshape(B, S * HA, 2 * DHA), v_f32.reshape(B, S * HA, DVA), lam_init, layer, cfg['tkv'])
    x_new, st = _mixer_tail(x_f32, qkm, vm_bf, om, ga, gb, gif, oa, conv_m_buf, m_state, ffn_buf, p, cfg)
    return x_new, (k_f32.reshape(B, S, HA, 2 * DHA), v_f32.reshape(B, S, HA, DVA)) + st


def _prompt_cfg(S):
    return dict(tm_proj=min(S, 512), tq=min(S, 2048), tk=min(S, 512), chunk=64, lc=min(S, 256),
                tm_ffn=min(S, 512))


def _sample_cfg(S, P):
    return dict(lc=S, tm_ffn=S, tkv=min(P, 512))


def kernel(x_prompt, x_sample, cache_k, cache_v, state_mlstm_conv, state_mlstm_C, state_mlstm_n,
           state_mlstm_m, state_ffn_conv, w_in, b_if, mlstm_conv_w, mlstm_conv_b, diff_lambda,
           diff_subln_g, mlstm_norm_g, w_out, ln1_g, ln1_b, w_up, ffn_conv_w, ffn_conv_b, w_down,
           ln2_g, ln2_b):
    bp, sp, _ = x_prompt.shape
    bs, ss, _ = x_sample.shape
    past = cache_k.shape[2]
    xp = (x_prompt, x_prompt)
    xs = (x_sample, x_sample)
    caches = (cache_k.reshape(DEPTH, bs, past * HA, 2 * DHA), cache_v.reshape(DEPTH, bs, past * HA, DVA))
    kv_all = None
    st_p, st_s = [], []
    for l in range(DEPTH):
        p = {'w_in': w_in[l], 'b_if': b_if[l], 'conv_m_w': mlstm_conv_w[l], 'conv_m_b': mlstm_conv_b[l],
             'lam': diff_lambda[l], 'subln_g': diff_subln_g[l], 'mh_g': mlstm_norm_g[l], 'w_out': w_out[l],
             'ln1_g': ln1_g[l], 'ln1_b': ln1_b[l], 'w_up': w_up[l], 'ffn_conv_w': ffn_conv_w[l],
             'ffn_conv_b': ffn_conv_b[l], 'w_down': w_down[l], 'ln2_g': ln2_g[l], 'ln2_b': ln2_b[l]}
        lam_init = 0.8 - 0.6 * math.exp(-0.3 * l)
        xp, kv_all, sp_l = _layer_prompt(xp, l, kv_all, p, lam_init, _prompt_cfg(sp))
        xs, ss_l = _layer_sample(xs, l, caches, state_mlstm_conv[l],
                                 (state_mlstm_C[l], state_mlstm_n[l], state_mlstm_m[l]), state_ffn_conv[l],
                                 p, lam_init, _sample_cfg(ss, past))
        st_p.append(sp_l)
        st_s.append(ss_l)
    stk = lambda lst, i: jnp.stack([s[i] for s in lst])
    k_prompt = kv_all[0].reshape(DEPTH, bp, sp, HA, 2 * DHA)
    v_prompt = kv_all[1].reshape(DEPTH, bp, sp, HA, DVA)
    return ((xp[0], xs[0], k_prompt, v_prompt) + tuple(stk(st_p, i) for i in range(5))
            + tuple(stk(st_s, i) for i in range(7)))
```

```python
import functools
import math

import jax
import jax.numpy as jnp
from jax import lax
from jax.experimental import pallas as pl
from jax.experimental.pallas import tpu as pltpu

F32 = jnp.float32
BF16 = jnp.bfloat16

D_MODEL = 1024
DEPTH = 2
PAST_LEN = 4096
HA = 8
DHA = 64
DVA = 2 * DHA
HM = 4
DKM = 256
DVM = 256
CONV_M = 4
D_FF = 2816
CONV_F = 3
ROPE_THETA = 10000.0
LN_EPS = 1e-5
ALPHA = (2 * DEPTH) ** 0.25

W_QA = HA * 2 * DHA
W_KA = HA * 2 * DHA
W_VA = HA * DVA
W_QKM = 2 * HM * DKM
W_VM = HM * DVM
W_OM = HM * DVM
W_GIF = 2 * HM
W_GA = HA * DVA
W_GB = HM * DVM
IN_SIZES = (W_QA, W_KA, W_VA, W_QKM, W_VM, W_OM, W_GIF, W_GA, W_GB)

LANES = 128
SUBLANES = 8
NEG_BIG = -1e30
LOG2E = math.log2(math.e)
Q_SCALE = DHA ** -0.5 * LOG2E
EXP_SPAN = 64.0
NORM_SLACK = 1.05
FFN_COLS = 256
VMEM_LIMIT = 52 * 1024 * 1024


def _cparams(sem):
    return pltpu.CompilerParams(dimension_semantics=sem, vmem_limit_bytes=VMEM_LIMIT)


def _resident(shape):
    nd = len(shape)
    return pl.BlockSpec(shape, lambda *_: (0,) * nd, pipeline_mode=pl.Buffered(1))


def _wspec(w):
    _, block, index = w
    return pl.BlockSpec(block, lambda *_: index, pipeline_mode=pl.Buffered(1))


def _layernorm_rows(r, g, b):
    mu = jnp.mean(r, axis=-1, keepdims=True)
    d = r - mu
    var = jnp.mean(d * d, axis=-1, keepdims=True)
    return d * lax.rsqrt(var + LN_EPS) * g + b


def _sigmoid(x):
    return 0.5 * jnp.tanh(0.5 * x) + 0.5


def _log_sigmoid(x):
    return jnp.minimum(x, 0.0) - jnp.log1p(jnp.exp(-jnp.abs(x)))


def _rope_fn(cos_ref, sin_ref):
    cos = cos_ref[...]
    sin = sin_ref[...]
    lane = lax.broadcasted_iota(jnp.int32, cos.shape, 1)
    first = (lane % DHA) < (DHA // 2)

    def rope(z):
        rot = jnp.where(first, pltpu.roll(z, LANES - DHA // 2, 1), pltpu.roll(z, DHA // 2, 1))
        return z * cos + rot * sin

    return rope


def _qkv_dots(x, w_ref, c0, chunk):
    zq = jnp.dot(x, w_ref[:, c0:c0 + chunk], preferred_element_type=F32)
    zk = jnp.dot(x, w_ref[:, W_QA + c0:W_QA + c0 + chunk], preferred_element_type=F32)
    zv = jnp.dot(x, w_ref[:, W_QA + W_KA + c0:W_QA + W_KA + c0 + chunk], preferred_element_type=F32)
    return zq, zk, zv


def _qkv_prompt_kernel(x_ref, w_ref, cos_ref, sin_ref, *refs):
    qt_ref, kf_ref, kb_ref, vf_ref, vt_ref = refs[-5:]
    for slot in range(1, kf_ref.shape[0]):
        kf_ref[slot] = jnp.zeros(kf_ref.shape[1:], F32)
        vf_ref[slot] = jnp.zeros(vf_ref.shape[1:], F32)
    x = x_ref[0].astype(BF16)
    rope = _rope_fn(cos_ref, sin_ref)
    chunk = 512
    for c0 in range(0, W_QA, chunk):
        zq, zk, zv = _qkv_dots(x, w_ref, c0, chunk)
        for g0 in range(0, chunk, LANES):
            h = (c0 + g0) // LANES
            sl = slice(c0 + g0, c0 + g0 + LANES)
            q = rope(zq[:, g0:g0 + LANES]) * Q_SCALE
            k = rope(zk[:, g0:g0 + LANES])
            v = zv[:, g0:g0 + LANES]
            qt_ref[0, h, 0] = q.T.astype(BF16)
            rows = pl.ds(h, x.shape[0], stride=HA)
            kf_ref[0, 0, rows, :] = k
            kb_ref[0, :, sl] = k.astype(BF16)
            vf_ref[0, 0, rows, :] = v
            vt_ref[0, h, 0] = v.T.astype(BF16)


def _qkv_proj_prompt(x, w_qkv, cos, sin, tm, layer, kv_all):
    B, S, D = x.shape
    ns = S // tm
    row = lambda b, s: (b, s, 0)
    tab = lambda b, s: (s, 0)
    nslot = DEPTH if kv_all is None else 1
    slot = lambda b, s: (0 if kv_all is None else layer, b, s, 0)
    tr = lambda b, s: (b, 0, s, 0, 0)
    in_specs = [pl.BlockSpec((1, tm, D), row), _wspec(w_qkv),
                pl.BlockSpec((tm, LANES), tab), pl.BlockSpec((tm, LANES), tab)]
    args = [x, w_qkv[0], cos, sin]
    aliases = {}
    if kv_all is not None:
        in_specs += [pl.BlockSpec(memory_space=pl.ANY)] * 2
        args += list(kv_all)
        aliases = {4: 1, 5: 3}
    t_shape = jax.ShapeDtypeStruct((B, HA, ns, DVA, tm), BF16)
    all_shape = jax.ShapeDtypeStruct((DEPTH, B, S * HA, DVA), F32)
    return pl.pallas_call(
        _qkv_prompt_kernel,
        grid=(B, ns),
        in_specs=in_specs,
        out_specs=[pl.BlockSpec((1, HA, 1, DVA, tm), tr), pl.BlockSpec((nslot, 1, tm * HA, DVA), slot),
                   pl.BlockSpec((1, tm, W_KA), row), pl.BlockSpec((nslot, 1, tm * HA, DVA), slot),
                   pl.BlockSpec((1, HA, 1, DVA, tm), tr)],
        out_shape=[t_shape, all_shape, jax.ShapeDtypeStruct((B, S, W_KA), BF16), all_shape, t_shape],
        input_output_aliases=aliases,
        compiler_params=_cparams(("parallel", "parallel")),
        name="qkv_rope_prompt",
    )(*args)


def _qkv_sample_kernel(x_ref, w_ref, cos_ref, sin_ref, q_ref, kf_ref, vf_ref):
    x = x_ref[0].astype(BF16)
    rope = _rope_fn(cos_ref, sin_ref)
    chunk = 512
    for c0 in range(0, W_QA, chunk):
        zq, zk, zv = _qkv_dots(x, w_ref, c0, chunk)
        for g0 in range(0, chunk, LANES):
            sl = slice(c0 + g0, c0 + g0 + LANES)
            q_ref[0, :, sl] = (rope(zq[:, g0:g0 + LANES]) * Q_SCALE).astype(BF16)
            kf_ref[0, :, sl] = rope(zk[:, g0:g0 + LANES])
        vf_ref[0, :, c0:c0 + chunk] = zv


def _qkv_proj_sample(x, w_qkv, cos, sin):
    B, M, D = x.shape
    out = lambda dt: jax.ShapeDtypeStruct((B, M, W_QA), dt)
    return pl.pallas_call(
        _qkv_sample_kernel,
        grid=(B,),
        in_specs=[pl.BlockSpec((1, M, D), lambda b: (b, 0, 0)), _wspec(w_qkv),
                  _resident(cos.shape), _resident(sin.shape)],
        out_specs=[pl.BlockSpec((1, M, W_QA), lambda b: (b, 0, 0))] * 3,
        out_shape=[out(BF16), out(F32), out(F32)],
        compiler_params=_cparams(("parallel",)),
        name="qkv_rope_sample",
    )(x, w_qkv[0], cos, sin)


def _proj_kernel(x_ref, *refs, plan):
    x = x_ref[0].astype(BF16)
    chunk = 512
    o_refs = refs[len(plan):]
    for w_ref, (oi, off) in zip(refs[:len(plan)], plan):
        n = w_ref.shape[1]
        o_ref = o_refs[oi]
        for c0 in range(0, n, chunk):
            c1 = min(c0 + chunk, n)
            o_ref[0, :, off + c0:off + c1] = jnp.dot(
                x, w_ref[:, c0:c1], preferred_element_type=F32).astype(o_ref.dtype)


def _proj(x, weights, plan, widths, dtypes, tm):
    B, S, D = x.shape
    row = lambda b, s: (b, s, 0)
    return pl.pallas_call(
        functools.partial(_proj_kernel, plan=plan),
        grid=(B, S // tm),
        in_specs=[pl.BlockSpec((1, tm, D), row)] + [_wspec(w) for w in weights],
        out_specs=[pl.BlockSpec((1, tm, n), row) for n in widths],
        out_shape=[jax.ShapeDtypeStruct((B, S, n), dt) for n, dt in zip(widths, dtypes)],
        compiler_params=_cparams(("parallel", "parallel")),
        name="mix_proj",
    )(x, *[w[0] for w in weights])


def _lambda(lp_ref, lam_init):
    lp = lp_ref[...]
    a = jnp.sum(lp[0:1] * lp[1:2], axis=1, keepdims=True)
    b = jnp.sum(lp[2:3] * lp[3:4], axis=1, keepdims=True)
    return jnp.exp(a) - jnp.exp(b) + lam_init


def _attn_prompt_kernel(lp_ref, g_ref, qt_ref, k_ref, vt_ref, o_ref, acc_ref, knorm_ref,
                        *, tq, tk, chunk, lam_init):
    qi = pl.program_id(2)
    r = tq // tk
    qt = jnp.concatenate([qt_ref[0, 0, t] for t in range(r)], axis=1)
    row = lax.broadcasted_iota(jnp.int32, qt.shape, 0)
    zero = jnp.zeros_like(qt)
    qq = jnp.concatenate([jnp.where(row < DHA, qt, zero), jnp.where(row >= DHA, qt, zero)], axis=1)
    acc_ref[...] = jnp.zeros_like(acc_ref)

    @pl.when(qi == 0)
    def _():
        kf = k_ref[0].astype(F32)
        dim = lax.broadcasted_iota(jnp.int32, (DVA, LANES), 0)
        col = lax.broadcasted_iota(jnp.int32, (DVA, LANES), 1)
        sel = jnp.where((dim // DHA) == col, 1.0, 0.0).astype(BF16)
        ksq = jnp.dot((kf * kf).astype(BF16), sel, preferred_element_type=F32)
        knorm_ref[...] = jnp.max(ksq, axis=0, keepdims=True)

    def both(x, c0):
        return x if c0 == 0 else jnp.concatenate([x[:, c0:tq], x[:, tq + c0:]], axis=1)

    def merge(old, new, c0):
        if c0 == 0:
            return new
        w = tq - c0
        return jnp.concatenate([old[:, :c0], new[:, :w], old[:, tq:tq + c0], new[:, w:]], axis=1)

    def scores(j, c0, mask):
        k = k_ref[0, pl.ds(pl.multiple_of(j * tk, tk), tk), :]
        s = jnp.dot(k, both(qq, c0), preferred_element_type=F32)
        if mask is not None:
            mk = mask[:, c0:]
            s = jnp.where(jnp.concatenate([mk, mk], axis=1), s, NEG_BIG)
        return s

    def accumulate(j, c0, p, scale):
        vt = vt_ref[0, 0, j]
        w = tq - c0
        pb = p.astype(BF16)
        for mp in range(2):
            pv = jnp.dot(vt, pb[:, mp * w:(mp + 1) * w], preferred_element_type=F32)
            old = acc_ref[mp, :, c0:]
            acc_ref[mp, :, c0:] = (old if scale is None else scale[:, mp * w:(mp + 1) * w] * old) + pv

    def running_max_step(j, carry, c0=0, mask=None):
        m, l = carry
        s = scores(j, c0, mask)
        mo = both(m, c0)
        n = jnp.maximum(mo, jnp.max(s, axis=0, keepdims=True))
        p = jnp.exp2(s - n)
        a = jnp.exp2(mo - n)
        ln = a * both(l, c0) + jnp.sum(p, axis=0, keepdims=True)
        accumulate(j, c0, p, a)
        return merge(m, n, c0), merge(l, ln, c0)

    def fixed_ref_step(j, carry, c0=0, mask=None):
        m, l = carry
        p = jnp.exp2(scores(j, c0, mask) - both(m, c0))
        accumulate(j, c0, p, None)
        return m, merge(l, both(l, c0) + jnp.sum(p, axis=0, keepdims=True), c0)

    nfull = qi * r

    def diag_mask(d):
        kr = (lax.broadcasted_iota(jnp.int32, (tk, tq), 0) + d * tk) // chunk
        qc = lax.broadcasted_iota(jnp.int32, (tk, tq), 1) // chunk
        return kr <= qc

    def sweep(step):
        def run(carry):
            for d in range(1, r):
                carry = step(nfull + d, carry, d * tk, diag_mask(d))
            return lax.fori_loop(0, nfull, step, carry)
        return run

    init = (jnp.full((1, 2 * tq), NEG_BIG, F32), jnp.zeros((1, 2 * tq), F32))
    carry = running_max_step(nfull, init, 0, diag_mask(0))

    qf = qt.astype(F32)
    qsq = qf * qf
    q1 = jnp.max(jnp.sum(qsq[:DHA], axis=0, keepdims=True), axis=1, keepdims=True)
    q2 = jnp.max(jnp.sum(qsq[DHA:], axis=0, keepdims=True), axis=1, keepdims=True)
    kn = knorm_ref[...]
    worst = jnp.maximum(q1 * kn[:, 0:1], q2 * kn[:, 1:2])
    safe = jnp.max(worst) * NORM_SLACK <= (0.5 * EXP_SPAN) ** 2
    _, l = lax.cond(safe, sweep(fixed_ref_step), sweep(running_max_step), carry)

    lam = _lambda(lp_ref, lam_init)
    o = acc_ref[0] / l[:, :tq] - lam * (acc_ref[1] / l[:, tq:])
    ms = jnp.mean(o * o, axis=0, keepdims=True)
    o = o * lax.rsqrt(ms + LN_EPS) * g_ref[...] * (1.0 - lam_init)
    o_ref[0] = o.T


def _attn_prompt(lp, g_col, qt, k_bf, vt, tq, chunk, lam_init):
    B, H, nt, _, tk = vt.shape
    S = k_bf.shape[1]
    r = tq // tk
    return pl.pallas_call(
        functools.partial(_attn_prompt_kernel, tq=tq, tk=tk, chunk=chunk, lam_init=lam_init),
        grid=(B, H, S // tq),
        in_specs=[_resident(lp.shape), _resident(g_col.shape),
                  pl.BlockSpec((1, 1, r, DVA, tk), lambda b, h, i: (b, h, i, 0, 0)),
                  pl.BlockSpec((1, S, DVA), lambda b, h, i: (b, 0, h)),
                  pl.BlockSpec((1, 1, nt, DVA, tk), lambda b, h, i: (b, h, 0, 0, 0))],
        out_specs=pl.BlockSpec((1, tq, DVA), lambda b, h, i: (b, i, h)),
        out_shape=jax.ShapeDtypeStruct((B, S, H * DVA), F32),
        scratch_shapes=[pltpu.VMEM((2, DVA, tq), F32), pltpu.VMEM((1, LANES), F32)],
        compiler_params=_cparams(("parallel", "parallel", "arbitrary")),
        name="diff_attn_prompt",
    )(lp, g_col, qt, k_bf, vt)


def _attn_sample_kernel(lp_ref, g_ref, q_ref, ck_ref, cv_ref, nk_ref, nv_ref, o_ref, qq_s, bias_s, m_s, l_s, acc_s,
                        *, lam_init):
    j = pl.program_id(1)
    L = q_ref.shape[1]
    nt = (((1,), (1,)), ((), ()))

    @pl.when(j == 0)
    def _():
        lane = lax.broadcasted_iota(jnp.int32, (L, DVA), 1)
        for mp in range(2):
            keep = (lane < DHA) if mp == 0 else (lane >= DHA)
            for h in range(HA):
                qh = q_ref[0, :, h * DVA:(h + 1) * DVA]
                qq_s[(mp * HA + h) * L:(mp * HA + h + 1) * L, :] = jnp.where(keep, qh, jnp.zeros_like(qh))
        row_head = (lax.broadcasted_iota(jnp.int32, bias_s.shape, 0) // L) % HA
        key_head = lax.broadcasted_iota(jnp.int32, bias_s.shape, 1) % HA
        bias_s[...] = jnp.where(row_head == key_head, 0.0, NEG_BIG)
        m_s[...] = jnp.full(m_s.shape, NEG_BIG, F32)
        l_s[...] = jnp.zeros_like(l_s)
        acc_s[...] = jnp.zeros_like(acc_s)

    def update(kf, vf):
        nk = kf.shape[0]
        s = lax.dot_general(qq_s[...], kf.astype(BF16), nt, preferred_element_type=F32) + bias_s[:, :nk]
        m = m_s[...]
        n = jnp.maximum(m, jnp.max(s, axis=1, keepdims=True))
        p = jnp.exp2(s - n)
        a = jnp.exp2(m - n)
        l_s[...] = a * l_s[...] + jnp.sum(p, axis=1, keepdims=True)
        acc_s[...] = a * acc_s[...] + jnp.dot(p.astype(BF16), vf.astype(BF16), preferred_element_type=F32)
        m_s[...] = n

    update(ck_ref[0, 0], cv_ref[0, 0])

    @pl.when(j == pl.num_programs(1) - 1)
    def _():
        update(nk_ref[0], nv_ref[0])
        lam = _lambda(lp_ref, lam_init)
        o = acc_s[...] / l_s[...]
        for h in range(HA):
            oh = o[h * L:(h + 1) * L] - lam * o[(HA + h) * L:(HA + h + 1) * L]
            ms = jnp.mean(oh * oh, axis=1, keepdims=True)
            o_ref[0, :, h * DVA:(h + 1) * DVA] = oh * lax.rsqrt(ms + LN_EPS) * g_ref[...] * (1.0 - lam_init)


def _attn_sample(lp, g_row, q_bf, ck_all, cv_all, nk, nv, lam_init, layer, tkv):
    B, L, _ = q_bf.shape
    rows = ck_all.shape[2]
    kb = tkv * HA
    per_b = lambda b, j: (b, 0, 0)
    blk = pl.BlockSpec((1, 1, kb, DVA), lambda b, j: (layer, b, j, 0))
    new = pl.BlockSpec((1, L * HA, DVA), per_b)
    nq = 2 * HA * L
    return pl.pallas_call(
        functools.partial(_attn_sample_kernel, lam_init=lam_init),
        grid=(B, rows // kb),
        in_specs=[_resident(lp.shape), _resident(g_row.shape), pl.BlockSpec((1, L, HA * DVA), per_b),
                  blk, blk, new, new],
        out_specs=pl.BlockSpec((1, L, HA * DVA), per_b),
        out_shape=jax.ShapeDtypeStruct((B, L, HA * DVA), F32),
        scratch_shapes=[pltpu.VMEM((nq, DVA), BF16), pltpu.VMEM((nq, kb), F32),
                        pltpu.VMEM((nq, 1), F32), pltpu.VMEM((nq, 1), F32), pltpu.VMEM((nq, DVA), F32)],
        compiler_params=_cparams(("parallel", "arbitrary")),
        name="diff_attn_sample",
    )(lp, g_row, q_bf, ck_all, cv_all, nk, nv)


def _mlstm_kernel(bif_ref, q_ref, k_ref, v_ref, om_ref, gif_ref, wq_ref, wk_ref, bq_ref, bk_ref,
                  cq_ref, ck_ref, c0_ref, n0_ref, m0_ref, mg_ref,
                  ga_ref, oa_ref, gb_ref, x_ref, wo_ref, lng_ref, lnb_ref,
                  xf_ref, xb_ref, c_ref, n_ref, m_ref, qpad, kpad, hm_s, *, lc, nb):
    c = pl.program_id(1)
    npre = CONV_M - 1
    off = SUBLANES - npre

    @pl.when(c == 0)
    def _():
        c_ref[...] = c0_ref[...]
        n_ref[...] = n0_ref[...]
        m_ref[...] = m0_ref[...]
        for r in range(nb):
            qpad[r, 0:SUBLANES, :] = jnp.zeros((SUBLANES, HM * DKM), F32)
            kpad[r, 0:SUBLANES, :] = jnp.zeros((SUBLANES, HM * DKM), F32)
            qpad[r, off:SUBLANES, :] = cq_ref[r]
            kpad[r, off:SUBLANES, :] = ck_ref[r]

    t_i = lax.broadcasted_iota(jnp.int32, (lc, lc), 0)
    s_i = lax.broadcasted_iota(jnp.int32, (lc, lc), 1)
    causal = s_i <= t_i
    zeros = jnp.zeros((lc, lc), F32)
    rows_t = -(-lc // LANES) * LANES

    for r in range(nb):
        qpad[r, SUBLANES:SUBLANES + lc, :] = q_ref[r]
        kpad[r, SUBLANES:SUBLANES + lc, :] = k_ref[r]

        def conv_silu(pad, w_ref, b_ref, cs):
            xa = pad[r, :, cs]
            y = b_ref[:, cs]
            for j in range(CONV_M):
                xs = pltpu.roll(xa, lc + SUBLANES - (off + j), 0)
                y = y + w_ref[j:j + 1, cs] * xs[0:lc, :]
            return y * _sigmoid(y)

        gpre = gif_ref[r]
        if rows_t != lc:
            gpre = jnp.concatenate([gpre, jnp.zeros((rows_t - lc, LANES), F32)], axis=0)
        grow = gpre.T[0:W_GIF, :] + bif_ref[...]
        is_forget = lax.broadcasted_iota(jnp.int32, grow.shape, 0) >= HM
        grow = jnp.where(is_forget, _log_sigmoid(grow), grow)
        gcol = jnp.concatenate([grow, jnp.zeros((LANES - W_GIF, rows_t), F32)], axis=0).T[0:lc, :]
        grow = grow[:, 0:lc]

        for h in range(HM):
            cs = slice(h * DKM, (h + 1) * DKM)
            q = conv_silu(qpad, wq_ref, bq_ref, cs)
            k = conv_silu(kpad, wk_ref, bk_ref, cs) * (DKM ** -0.5)

            ig_c = gcol[:, h:h + 1]
            lf_c = gcol[:, HM + h:HM + h + 1]
            ig_r = grow[h:h + 1, :]
            lf_r = grow[HM + h:HM + h + 1, :]
            bcum_c = jnp.sum(jnp.where(causal, lf_r, zeros), axis=1, keepdims=True)
            bcum_r = jnp.sum(jnp.where(t_i <= s_i, lf_c, zeros), axis=0, keepdims=True)

            m_prev = m_ref[r, h][:, 0:1]
            d = jnp.where(causal, bcum_c - bcum_r + ig_r, NEG_BIG)
            inter = bcum_c + m_prev
            m_t = jnp.maximum(inter, jnp.max(d, axis=1, keepdims=True))
            w = jnp.exp(d - m_t)
            g = jnp.exp(inter - m_t)

            qb = q.astype(BF16)
            kb = k.astype(BF16)
            vb = v_ref[r, :, cs]
            cmat = c_ref[r, h]
            nrow = n_ref[r, h]
            s = lax.dot_general(qb, kb, (((1,), (1,)), ((), ())), preferred_element_type=F32) * w
            num = (jnp.dot(s.astype(BF16), vb, preferred_element_type=F32)
                   + g * jnp.dot(qb, cmat.astype(BF16), preferred_element_type=F32))
            den = jnp.sum(s, axis=1, keepdims=True) + g * jnp.sum(q * nrow, axis=1, keepdims=True)
            den = jnp.maximum(jnp.abs(den), jnp.exp(-m_t))
            hh = num / den

            m_end = m_t[lc - 1:lc, :]
            b_last = bcum_c[lc - 1:lc, :]
            w_end = jnp.exp(b_last - bcum_c + ig_c - m_end)
            g_end = jnp.exp(b_last + m_prev - m_end)
            kw = k * w_end
            c_ref[r, h] = g_end * cmat + lax.dot_general(kw.astype(BF16), vb, (((0,), (0,)), ((), ())),
                                                         preferred_element_type=F32)
            n_ref[r, h] = g_end * nrow + jnp.sum(kw, axis=0, keepdims=True)
            m_ref[r, h] = jnp.broadcast_to(m_end, (1, LANES))

            mu = jnp.mean(hh, axis=1, keepdims=True)
            hc = hh - mu
            var = jnp.mean(hc * hc, axis=1, keepdims=True)
            hm_s[r, :, cs] = hc * lax.rsqrt(var + LN_EPS) * mg_ref[:, cs] * _sigmoid(om_ref[r, :, cs])

        qpad[r, 0:SUBLANES, :] = qpad[r, lc:lc + SUBLANES, :]
        kpad[r, 0:SUBLANES, :] = kpad[r, lc:lc + SUBLANES, :]

        y = _sigmoid(ga_ref[r]) * oa_ref[r] + _sigmoid(gb_ref[r]) * hm_s[r]
        z = jnp.dot(y.astype(BF16), wo_ref[...], preferred_element_type=F32)
        out = _layernorm_rows(ALPHA * x_ref[r] + z, lng_ref[...], lnb_ref[...])
        xf_ref[r] = out
        xb_ref[r] = out.astype(BF16)


def _mlstm_merge(b_if, qkm, vm_bf, om, gif, conv_w, conv_b, conv_state, c0, n0, m0, mh_g,
                 ga, oa, gb, x, w_out, ln_g, ln_b, lc, nb):
    B, S, D = x.shape
    nc = S // lc
    npre = CONV_M - 1
    wq, wv = HM * DKM, HM * DVM
    rowq = lambda b, c: (b, c, 0)
    rowk = lambda b, c: (b, c, 1)
    st = lambda b, c: (b, 0, 0, 0)
    blk = pl.BlockSpec((nb, lc, D), rowq)
    return pl.pallas_call(
        functools.partial(_mlstm_kernel, lc=lc, nb=nb),
        grid=(B // nb, nc),
        in_specs=[_resident(b_if.shape),
                  pl.BlockSpec((nb, lc, wq), rowq), pl.BlockSpec((nb, lc, wq), rowk),
                  pl.BlockSpec((nb, lc, wv), rowq), pl.BlockSpec((nb, lc, wv), rowq),
                  pl.BlockSpec((nb, lc, LANES), rowq),
                  pl.BlockSpec((CONV_M, wq), lambda b, c: (0, 0)), pl.BlockSpec((CONV_M, wq), lambda b, c: (0, 1)),
                  pl.BlockSpec((1, wq), lambda b, c: (0, 0)), pl.BlockSpec((1, wq), lambda b, c: (0, 1)),
                  pl.BlockSpec((nb, npre, wq), lambda b, c: (b, 0, 0)),
                  pl.BlockSpec((nb, npre, wq), lambda b, c: (b, 0, 1)),
                  pl.BlockSpec((nb, HM, DKM, DVM), st), pl.BlockSpec((nb, HM, 1, DKM), st),
                  pl.BlockSpec((nb, HM, 1, LANES), st), _resident(mh_g.shape),
                  blk, blk, blk, blk, _wspec(w_out), _resident(ln_g.shape), _resident(ln_b.shape)],
        out_specs=[blk, blk,
                   pl.BlockSpec((nb, HM, DKM, DVM), st), pl.BlockSpec((nb, HM, 1, DKM), st),
                   pl.BlockSpec((nb, HM, 1, LANES), st)],
        out_shape=[jax.ShapeDtypeStruct((B, S, D), F32), jax.ShapeDtypeStruct((B, S, D), BF16),
                   jax.ShapeDtypeStruct((B, HM, DKM, DVM), F32),
                   jax.ShapeDtypeStruct((B, HM, 1, DKM), F32),
                   jax.ShapeDtypeStruct((B, HM, 1, LANES), F32)],
        scratch_shapes=[pltpu.VMEM((nb, lc + SUBLANES, wq), F32), pltpu.VMEM((nb, lc + SUBLANES, wq), F32),
                        pltpu.VMEM((nb, lc, wv), F32)],
        compiler_params=_cparams(("parallel", "arbitrary")),
        name="mlstm_merge_ln",
    )(b_if, qkm, qkm, vm_bf, om, gif, conv_w, conv_w, conv_b, conv_b, conv_state, conv_state,
      c0, n0, m0, mh_g, ga, oa, gb, x, w_out[0], ln_g, ln_b)


def _ffn_kernel(xb_ref, xf_ref, wu_ref, cw_ref, cb_ref, buf_ref, wd_ref, g_ref, b_ref,
                of_ref, ob_ref, tail_ref, upad, *, seg):
    nseg = xb_ref.shape[0]
    npre = CONV_F - 1
    off = SUBLANES - npre
    stride = seg + SUBLANES

    @pl.when(pl.program_id(1) == 0)
    def _():
        for r in range(nseg):
            upad[r * stride:r * stride + SUBLANES, :] = jnp.zeros((SUBLANES, 2 * D_FF), F32)
            upad[r * stride + off:r * stride + SUBLANES, :] = buf_ref[r]

    x = xb_ref[...].reshape(nseg * seg, D_MODEL)
    chunk = 512
    for c0 in range(0, 2 * D_FF, chunk):
        u = jnp.dot(x, wu_ref[:, c0:c0 + chunk], preferred_element_type=F32)
        for r in range(nseg):
            upad[r * stride + SUBLANES:(r + 1) * stride, c0:c0 + chunk] = u[r * seg:(r + 1) * seg]

    def conv(c0, c1):
        rows = []
        for r in range(nseg):
            y = cb_ref[:, c0:c1]
            for j in range(CONV_F):
                y = y + cw_ref[j:j + 1, c0:c1] * upad[r * stride + off + j:r * stride + off + j + seg, c0:c1]
            rows.append(y)
        return rows[0] if nseg == 1 else jnp.concatenate(rows, axis=0)

    z = None
    for c0 in range(0, D_FF, FFN_COLS):
        a = conv(c0, c0 + FFN_COLS)
        gate = conv(D_FF + c0, D_FF + c0 + FFN_COLS)
        act = 0.5 * a * (1.0 + lax.erf(a * (2.0 ** -0.5)))
        zc = jnp.dot((act * gate).astype(BF16), wd_ref[c0:c0 + FFN_COLS, :], preferred_element_type=F32)
        z = zc if z is None else z + zc

    for r in range(nseg):
        tail = upad[(r + 1) * stride - SUBLANES:(r + 1) * stride, :]
        tail_ref[r] = tail
        upad[r * stride:r * stride + SUBLANES, :] = tail

    out = _layernorm_rows(ALPHA * xf_ref[...].reshape(nseg * seg, D_MODEL) + z, g_ref[...], b_ref[...])
    of_ref[...] = out.reshape(nseg, seg, D_MODEL)
    ob_ref[...] = out.astype(BF16).reshape(nseg, seg, D_MODEL)


def _ffn(x_bf, x_f32, w_up, conv_w, conv_b, buf, w_down, g, b, tm, nseg):
    B, S, D = x_f32.shape
    npre = CONV_F - 1
    row = lambda bb, s: (bb, s, 0)
    blk = pl.BlockSpec((nseg, tm, D), row)
    per_b = lambda bb, s: (bb, 0, 0)
    return pl.pallas_call(
        functools.partial(_ffn_kernel, seg=tm),
        grid=(B // nseg, S // tm),
        in_specs=[blk, blk, _wspec(w_up), _resident(conv_w.shape), _resident(conv_b.shape),
                  pl.BlockSpec((nseg, npre, 2 * D_FF), per_b), _wspec(w_down),
                  _resident(g.shape), _resident(b.shape)],
        out_specs=[blk, blk, pl.BlockSpec((nseg, SUBLANES, 2 * D_FF), per_b)],
        out_shape=[jax.ShapeDtypeStruct((B, S, D), F32), jax.ShapeDtypeStruct((B, S, D), BF16),
                   jax.ShapeDtypeStruct((B, SUBLANES, 2 * D_FF), F32)],
        scratch_shapes=[pltpu.VMEM((nseg * (tm + SUBLANES), 2 * D_FF), F32)],
        compiler_params=_cparams(("parallel", "arbitrary")),
        name="convffn_ln",
    )(x_bf, x_f32, w_up[0], conv_w, conv_b, buf, w_down[0], g, b)


def _rope_tables(pos):
    half = DHA // 2
    inv = ROPE_THETA ** (-jnp.arange(half, dtype=F32) * 2.0 / DHA)
    ang = pos.astype(F32)[:, None] * inv[None, :]
    cos = jnp.tile(jnp.cos(ang), (1, LANES // half))
    sin = jnp.tile(jnp.sin(ang), (1, LANES // half))
    first = (jnp.arange(LANES) % DHA) < half
    return cos, jnp.where(first[None, :], -sin, sin)


def _prepare_weights(w_in, w_out, w_up, w_down):
    offs = [0]
    for n in IN_SIZES:
        offs.append(offs[-1] + n)
    w_in = w_in.astype(BF16)
    ga_w = w_in[:, :, offs[7]:offs[8]]
    gb_w = w_in[:, :, offs[8]:offs[9]]
    gif_w = jnp.pad(w_in[:, :, offs[6]:offs[7]], ((0, 0), (0, 0), (0, LANES - W_GIF)))
    w_out, w_up, w_down = w_out.astype(BF16), w_up.astype(BF16), w_down.astype(BF16)
    cb = D_MODEL
    assert all(offs[i] % cb == 0 for i in (3, 4, 5)) and W_QKM == 2 * cb and W_VM == cb and W_OM == cb
    layers = []
    for l in range(w_in.shape[0]):
        blk = lambda j, l=l: (w_in, (None, D_MODEL, cb), (l, 0, j))
        whole = lambda w, l=l: (w, (None,) + w.shape[1:], (l, 0, 0))
        layers.append(dict(
            qkv=(w_in, (None, D_MODEL, offs[3]), (l, 0, 0)),
            mix=[blk(offs[3] // cb), blk(offs[3] // cb + 1), blk(offs[4] // cb), blk(offs[5] // cb),
                 whole(ga_w), whole(gb_w), whole(gif_w)],
            out=whole(w_out), up=whole(w_up), down=whole(w_down)))
    return layers


MIX_PLAN = ((0, 0), (0, D_MODEL), (1, 0), (2, 0), (3, 0), (4, 0), (5, 0))
MIX_WIDTHS = (W_QKM, W_VM, W_OM, W_GA, W_GB, LANES)
MIX_DTYPES = (F32, BF16, F32, F32, F32, F32)


def _mixer_tail(x_f32, qkm, vm_bf, om, ga, gb, gif, oa, conv_m_buf, m_state, ffn_buf, p, cfg):
    B, S, _ = x_f32.shape
    if m_state is None:
        c0 = jnp.zeros((B, HM, DKM, DVM), F32)
        n0 = jnp.zeros((B, HM, 1, DKM), F32)
        m0 = jnp.zeros((B, HM, 1, LANES), F32)
    else:
        c0 = m_state[0].astype(F32)
        n0 = m_state[1].astype(F32).reshape(B, HM, 1, DKM)
        m0 = jnp.broadcast_to(m_state[2].astype(F32)[:, :, None, None], (B, HM, 1, LANES))
    row = lambda a: a.reshape(1, -1)
    x1_f32, x1_bf, c_new, n_new, m_new = _mlstm_merge(
        p['b_if'].astype(F32).reshape(W_GIF, 1), qkm, vm_bf, om, gif, p['conv_m_w'], row(p['conv_m_b']), conv_m_buf,
        c0, n0, m0, row(p['mh_g']), ga, oa, gb, x_f32, p['w']['out'], row(p['ln1_g']), row(p['ln1_b']),
        cfg['lc'], cfg['nb'])
    new_conv_m = jnp.concatenate([conv_m_buf, qkm[:, S - min(S, CONV_M - 1):]], axis=1)[:, -(CONV_M - 1):]

    x2_f32, x2_bf, tail = _ffn(x1_bf, x1_f32, p['w']['up'], p['ffn_conv_w'], row(p['ffn_conv_b']),
                               ffn_buf, p['w']['down'], row(p['ln2_g']), row(p['ln2_b']),
                               cfg['tm_ffn'], cfg['nb'])
    new_ffn = tail[:, SUBLANES - (CONV_F - 1):]
    return (x2_f32, x2_bf), (new_conv_m, c_new, n_new.reshape(B, HM, DKM), m_new[:, :, 0, 0], new_ffn)


def _layer_prompt(x, layer, kv_all, p, lam_init, cfg):
    x_f32, x_mm = x
    B, S, _ = x_f32.shape
    cos, sin = _rope_tables(jnp.arange(S))
    qt, k_all, k_bf, v_all, vt = _qkv_proj_prompt(x_mm, p['w']['qkv'], cos, sin, cfg['tk'], layer, kv_all)
    qkm, vm_bf, om, ga, gb, gif = _proj(x_mm, p['w']['mix'], MIX_PLAN, MIX_WIDTHS, MIX_DTYPES, cfg['tm_proj'])
    oa = _attn_prompt(p['lam'].astype(F32), p['subln_g'].reshape(DVA, 1), qt, k_bf, vt, cfg['tq'], cfg['chunk'],
                      lam_init)
    conv_m0 = jnp.zeros((B, CONV_M - 1, W_QKM), F32)
    ffn0 = jnp.zeros((B, CONV_F - 1, 2 * D_FF), F32)
    x_new, st = _mixer_tail(x_f32, qkm, vm_bf, om, ga, gb, gif, oa, conv_m0, None, ffn0, p, cfg)
    return x_new, (k_all, v_all), st


def _layer_sample(x, layer, caches, conv_m_buf, m_state, ffn_buf, p, lam_init, cfg):
    x_f32, x_mm = x
    B, S, _ = x_f32.shape
    cos, sin = _rope_tables(PAST_LEN + jnp.arange(S))
    cos, sin = jnp.tile(cos, (B, 1)), jnp.tile(sin, (B, 1))
    xin = x_mm.reshape(1, B * S, D_MODEL)
    unflat = lambda a: a.reshape(B, S, a.shape[-1])
    q_bf, k_f32, v_f32 = map(unflat, _qkv_proj_sample(xin, p['w']['qkv'], cos, sin))
    qkm, vm_bf, om, ga, gb, gif = map(unflat, _proj(xin, p['w']['mix'], MIX_PLAN, MIX_WIDTHS, MIX_DTYPES, B * S))
    oa = _attn_sample(p['lam'].astype(F32), p['subln_g'].reshape(1, DVA), q_bf, caches[0], caches[1],
                      k_f32.reshape(B, S * HA, 2 * DHA), v_f32.reshape(B, S * HA, DVA), lam_init, layer, cfg['tkv'])
    x_new, st = _mixer_tail(x_f32, qkm, vm_bf, om, ga, gb, gif, oa, conv_m_buf, m_state, ffn_buf, p, cfg)
    return x_new, (k_f32.reshape(B, S, HA, 2 * DHA), v_f32.reshape(B, S, HA, DVA)) + st


def _prompt_cfg(S):
    return dict(tm_proj=min(S, 512), tq=min(S, 2048), tk=min(S, 512), chunk=64, lc=min(S, 256), nb=1,
                tm_ffn=min(S, 512))


def _sample_cfg(S, P, B):
    return dict(lc=S, nb=B, tm_ffn=S, tkv=min(P, 512))


def kernel(x_prompt, x_sample, cache_k, cache_v, state_mlstm_conv, state_mlstm_C, state_mlstm_n,
           state_mlstm_m, state_ffn_conv, w_in, b_if, mlstm_conv_w, mlstm_conv_b, diff_lambda,
           diff_subln_g, mlstm_norm_g, w_out, ln1_g, ln1_b, w_up, ffn_conv_w, ffn_conv_b, w_down,
           ln2_g, ln2_b):
    bp, sp, _ = x_prompt.shape
    bs, ss, _ = x_sample.shape
    past = cache_k.shape[2]
    xp = (x_prompt, x_prompt)
    xs = (x_sample, x_sample)
    caches = (cache_k.reshape(DEPTH, bs, past * HA, 2 * DHA), cache_v.reshape(DEPTH, bs, past * HA, DVA))
    weights = _prepare_weights(w_in, w_out, w_up, w_down)
    kv_all = None
    st_p, st_s = [], []
    for l in range(DEPTH):
        p = {'w': weights[l], 'b_if': b_if[l], 'conv_m_w': mlstm_conv_w[l], 'conv_m_b': mlstm_conv_b[l],
             'lam': diff_lambda[l], 'subln_g': diff_subln_g[l], 'mh_g': mlstm_norm_g[l],
             'ln1_g': ln1_g[l], 'ln1_b': ln1_b[l], 'ffn_conv_w': ffn_conv_w[l],
             'ffn_conv_b': ffn_conv_b[l], 'ln2_g': ln2_g[l], 'ln2_b': ln2_b[l]}
        lam_init = 0.8 - 0.6 * math.exp(-0.3 * l)
        xp, kv_all, sp_l = _layer_prompt(xp, l, kv_all, p, lam_init, _prompt_cfg(sp))
        xs, ss_l = _layer_sample(xs, l, caches, state_mlstm_conv[l],
                                 (state_mlstm_C[l], state_mlstm_n[l], state_mlstm_m[l]), state_ffn_conv[l],
                                 p, lam_init, _sample_cfg(ss, past, bs))
        st_p.append(sp_l)
        st_s.append(ss_l)
    stk = lambda lst, i: jnp.stack([s[i] for s in lst])
    k_prompt = kv_all[0].reshape(DEPTH, bp, sp, HA, 2 * DHA)
    v_prompt = kv_all[1].reshape(DEPTH, bp, sp, HA, DVA)
    return ((xp[0], xs[0], k_prompt, v_prompt) + tuple(stk(st_p, i) for i in range(5))
            + tuple(stk(st_s, i) for i in range(7)))
```

```python
import functools
import math

import jax
import jax.numpy as jnp
from jax import lax
from jax.experimental import pallas as pl
from jax.experimental.pallas import tpu as pltpu

F32 = jnp.float32
BF16 = jnp.bfloat16

D_MODEL = 1024
DEPTH = 2
PAST_LEN = 4096
HA = 8
DHA = 64
DVA = 2 * DHA
HM = 4
DKM = 256
DVM = 256
CONV_M = 4
D_FF = 2816
CONV_F = 3
ROPE_THETA = 10000.0
LN_EPS = 1e-5
ALPHA = (2 * DEPTH) ** 0.25

W_QA = HA * 2 * DHA
W_KA = HA * 2 * DHA
W_VA = HA * DVA
W_QKM = 2 * HM * DKM
W_VM = HM * DVM
W_OM = HM * DVM
W_GIF = 2 * HM
W_GA = HA * DVA
W_GB = HM * DVM
IN_SIZES = (W_QA, W_KA, W_VA, W_QKM, W_VM, W_OM, W_GIF, W_GA, W_GB)

LANES = 128
SUBLANES = 8
NEG_BIG = -1e30
LOG2E = math.log2(math.e)
Q_SCALE = DHA ** -0.5 * LOG2E
EXP_SPAN = 64.0
NORM_SLACK = 1.05
FFN_COLS = 256
VMEM_LIMIT = 52 * 1024 * 1024


def _cparams(sem):
    return pltpu.CompilerParams(dimension_semantics=sem, vmem_limit_bytes=VMEM_LIMIT)


def _resident(shape):
    nd = len(shape)
    return pl.BlockSpec(shape, lambda *_: (0,) * nd, pipeline_mode=pl.Buffered(1))


def _wspec(w):
    _, block, index = w
    return pl.BlockSpec(block, lambda *_: index, pipeline_mode=pl.Buffered(1))


def _layernorm_rows(r, g, b):
    mu = jnp.mean(r, axis=-1, keepdims=True)
    d = r - mu
    var = jnp.mean(d * d, axis=-1, keepdims=True)
    return d * lax.rsqrt(var + LN_EPS) * g + b


def _sigmoid(x):
    return 0.5 * jnp.tanh(0.5 * x) + 0.5


def _log_sigmoid(x):
    return jnp.minimum(x, 0.0) - jnp.log1p(jnp.exp(-jnp.abs(x)))


def _rope_fn(cos_ref, sin_ref):
    cos = cos_ref[...]
    sin = sin_ref[...]
    lane = lax.broadcasted_iota(jnp.int32, cos.shape, 1)
    first = (lane % DHA) < (DHA // 2)

    def rope(z):
        rot = jnp.where(first, pltpu.roll(z, LANES - DHA // 2, 1), pltpu.roll(z, DHA // 2, 1))
        return z * cos + rot * sin

    return rope


def _qkv_dots(x, w_ref, c0, chunk):
    zq = jnp.dot(x, w_ref[:, c0:c0 + chunk], preferred_element_type=F32)
    zk = jnp.dot(x, w_ref[:, W_QA + c0:W_QA + c0 + chunk], preferred_element_type=F32)
    zv = jnp.dot(x, w_ref[:, W_QA + W_KA + c0:W_QA + W_KA + c0 + chunk], preferred_element_type=F32)
    return zq, zk, zv


def _qkv_prompt_kernel(x_ref, w_ref, cos_ref, sin_ref, *refs):
    qt_ref, kf_ref, kb_ref, vf_ref, vt_ref = refs[-5:]
    for slot in range(1, kf_ref.shape[0]):
        kf_ref[slot] = jnp.zeros(kf_ref.shape[1:], F32)
        vf_ref[slot] = jnp.zeros(vf_ref.shape[1:], F32)
    x = x_ref[0].astype(BF16)
    rope = _rope_fn(cos_ref, sin_ref)
    chunk = 512
    for c0 in range(0, W_QA, chunk):
        zq, zk, zv = _qkv_dots(x, w_ref, c0, chunk)
        for g0 in range(0, chunk, LANES):
            h = (c0 + g0) // LANES
            sl = slice(c0 + g0, c0 + g0 + LANES)
            q = rope(zq[:, g0:g0 + LANES]) * Q_SCALE
            k = rope(zk[:, g0:g0 + LANES])
            v = zv[:, g0:g0 + LANES]
            qt_ref[0, h, 0] = q.T.astype(BF16)
            rows = pl.ds(h, x.shape[0], stride=HA)
            kf_ref[0, 0, rows, :] = k
            kb_ref[0, :, sl] = k.astype(BF16)
            vf_ref[0, 0, rows, :] = v
            vt_ref[0, h, 0] = v.T.astype(BF16)


def _qkv_proj_prompt(x, w_qkv, cos, sin, tm, layer, kv_all):
    B, S, D = x.shape
    ns = S // tm
    row = lambda b, s: (b, s, 0)
    tab = lambda b, s: (s, 0)
    nslot = DEPTH if kv_all is None else 1
    slot = lambda b, s: (0 if kv_all is None else layer, b, s, 0)
    tr = lambda b, s: (b, 0, s, 0, 0)
    in_specs = [pl.BlockSpec((1, tm, D), row), _wspec(w_qkv),
                pl.BlockSpec((tm, LANES), tab), pl.BlockSpec((tm, LANES), tab)]
    args = [x, w_qkv[0], cos, sin]
    aliases = {}
    if kv_all is not None:
        in_specs += [pl.BlockSpec(memory_space=pl.ANY)] * 2
        args += list(kv_all)
        aliases = {4: 1, 5: 3}
    t_shape = jax.ShapeDtypeStruct((B, HA, ns, DVA, tm), BF16)
    all_shape = jax.ShapeDtypeStruct((DEPTH, B, S * HA, DVA), F32)
    return pl.pallas_call(
        _qkv_prompt_kernel,
        grid=(B, ns),
        in_specs=in_specs,
        out_specs=[pl.BlockSpec((1, HA, 1, DVA, tm), tr), pl.BlockSpec((nslot, 1, tm * HA, DVA), slot),
                   pl.BlockSpec((1, tm, W_KA), row), pl.BlockSpec((nslot, 1, tm * HA, DVA), slot),
                   pl.BlockSpec((1, HA, 1, DVA, tm), tr)],
        out_shape=[t_shape, all_shape, jax.ShapeDtypeStruct((B, S, W_KA), BF16), all_shape, t_shape],
        input_output_aliases=aliases,
        compiler_params=_cparams(("parallel", "parallel")),
        name="qkv_rope_prompt",
    )(*args)


def _qkv_sample_kernel(x_ref, w_ref, cos_ref, sin_ref, q_ref, kf_ref, vf_ref):
    x = x_ref[0].astype(BF16)
    rope = _rope_fn(cos_ref, sin_ref)
    chunk = 512
    for c0 in range(0, W_QA, chunk):
        zq, zk, zv = _qkv_dots(x, w_ref, c0, chunk)
        for g0 in range(0, chunk, LANES):
            sl = slice(c0 + g0, c0 + g0 + LANES)
            q_ref[0, :, sl] = (rope(zq[:, g0:g0 + LANES]) * Q_SCALE).astype(BF16)
            kf_ref[0, :, sl] = rope(zk[:, g0:g0 + LANES])
        vf_ref[0, :, c0:c0 + chunk] = zv


def _qkv_proj_sample(x, w_qkv, cos, sin):
    B, M, D = x.shape
    out = lambda dt: jax.ShapeDtypeStruct((B, M, W_QA), dt)
    return pl.pallas_call(
        _qkv_sample_kernel,
        grid=(B,),
        in_specs=[pl.BlockSpec((1, M, D), lambda b: (b, 0, 0)), _wspec(w_qkv),
                  _resident(cos.shape), _resident(sin.shape)],
        out_specs=[pl.BlockSpec((1, M, W_QA), lambda b: (b, 0, 0))] * 3,
        out_shape=[out(BF16), out(F32), out(F32)],
        compiler_params=_cparams(("parallel",)),
        name="qkv_rope_sample",
    )(x, w_qkv[0], cos, sin)


def _proj_kernel(x_ref, *refs, plan):
    x = x_ref[0].astype(BF16)
    chunk = 512
    o_refs = refs[len(plan):]
    for w_ref, (oi, off) in zip(refs[:len(plan)], plan):
        n = w_ref.shape[1]
        o_ref = o_refs[oi]
        for c0 in range(0, n, chunk):
            c1 = min(c0 + chunk, n)
            o_ref[0, :, off + c0:off + c1] = jnp.dot(
                x, w_ref[:, c0:c1], preferred_element_type=F32).astype(o_ref.dtype)


def _proj(x, weights, plan, widths, dtypes, tm):
    B, S, D = x.shape
    row = lambda b, s: (b, s, 0)
    return pl.pallas_call(
        functools.partial(_proj_kernel, plan=plan),
        grid=(B, S // tm),
        in_specs=[pl.BlockSpec((1, tm, D), row)] + [_wspec(w) for w in weights],
        out_specs=[pl.BlockSpec((1, tm, n), row) for n in widths],
        out_shape=[jax.ShapeDtypeStruct((B, S, n), dt) for n, dt in zip(widths, dtypes)],
        compiler_params=_cparams(("parallel", "parallel")),
        name="mix_proj",
    )(x, *[w[0] for w in weights])


def _lambda(lp_ref, lam_init):
    lp = lp_ref[...]
    a = jnp.sum(lp[0:1] * lp[1:2], axis=1, keepdims=True)
    b = jnp.sum(lp[2:3] * lp[3:4], axis=1, keepdims=True)
    return jnp.exp(a) - jnp.exp(b) + lam_init


def _attn_prompt_kernel(lp_ref, g_ref, qt_ref, k_ref, vt_ref, o_ref, acc_ref, knorm_ref,
                        *, tq, tk, chunk, lam_init):
    qi = pl.program_id(2)
    r = tq // tk
    qt = jnp.concatenate([qt_ref[0, 0, t] for t in range(r)], axis=1)
    row = lax.broadcasted_iota(jnp.int32, qt.shape, 0)
    zero = jnp.zeros_like(qt)
    qq = jnp.concatenate([jnp.where(row < DHA, qt, zero), jnp.where(row >= DHA, qt, zero)], axis=1)
    acc_ref[...] = jnp.zeros_like(acc_ref)

    @pl.when(qi == 0)
    def _():
        kf = k_ref[0].astype(F32)
        dim = lax.broadcasted_iota(jnp.int32, (DVA, LANES), 0)
        col = lax.broadcasted_iota(jnp.int32, (DVA, LANES), 1)
        sel = jnp.where((dim // DHA) == col, 1.0, 0.0).astype(BF16)
        ksq = jnp.dot((kf * kf).astype(BF16), sel, preferred_element_type=F32)
        knorm_ref[...] = jnp.max(ksq, axis=0, keepdims=True)

    def both(x, c0):
        return x if c0 == 0 else jnp.concatenate([x[:, c0:tq], x[:, tq + c0:]], axis=1)

    def merge(old, new, c0):
        if c0 == 0:
            return new
        w = tq - c0
        return jnp.concatenate([old[:, :c0], new[:, :w], old[:, tq:tq + c0], new[:, w:]], axis=1)

    def scores(j, c0, mask):
        k = k_ref[0, pl.ds(pl.multiple_of(j * tk, tk), tk), :]
        s = jnp.dot(k, both(qq, c0), preferred_element_type=F32)
        if mask is not None:
            mk = mask[:, c0:]
            s = jnp.where(jnp.concatenate([mk, mk], axis=1), s, NEG_BIG)
        return s

    def accumulate(j, c0, p, scale):
        vt = vt_ref[0, 0, j]
        w = tq - c0
        pb = p.astype(BF16)
        for mp in range(2):
            pv = jnp.dot(vt, pb[:, mp * w:(mp + 1) * w], preferred_element_type=F32)
            old = acc_ref[mp, :, c0:]
            acc_ref[mp, :, c0:] = (old if scale is None else scale[:, mp * w:(mp + 1) * w] * old) + pv

    def running_max_step(j, carry, c0=0, mask=None):
        m, l = carry
        s = scores(j, c0, mask)
        mo = both(m, c0)
        n = jnp.maximum(mo, jnp.max(s, axis=0, keepdims=True))
        p = jnp.exp2(s - n)
        a = jnp.exp2(mo - n)
        ln = a * both(l, c0) + jnp.sum(p, axis=0, keepdims=True)
        accumulate(j, c0, p, a)
        return merge(m, n, c0), merge(l, ln, c0)

    def fixed_ref_step(j, carry, c0=0, mask=None):
        m, l = carry
        p = jnp.exp2(scores(j, c0, mask) - both(m, c0))
        accumulate(j, c0, p, None)
        return m, merge(l, both(l, c0) + jnp.sum(p, axis=0, keepdims=True), c0)

    nfull = qi * r

    def diag_mask(d):
        kr = (lax.broadcasted_iota(jnp.int32, (tk, tq), 0) + d * tk) // chunk
        qc = lax.broadcasted_iota(jnp.int32, (tk, tq), 1) // chunk
        return kr <= qc

    def sweep(step):
        def run(carry):
            for d in range(r):
                carry = step(nfull + d, carry, d * tk, diag_mask(d))
            return lax.fori_loop(0, nfull, step, carry)
        return run

    qf = qt.astype(F32)
    qsq = qf * qf
    q1 = jnp.max(jnp.sum(qsq[:DHA], axis=0, keepdims=True), axis=1, keepdims=True)
    q2 = jnp.max(jnp.sum(qsq[DHA:], axis=0, keepdims=True), axis=1, keepdims=True)
    kn = knorm_ref[...]
    worst = jnp.maximum(q1 * kn[:, 0:1], q2 * kn[:, 1:2])
    safe = jnp.max(worst) * NORM_SLACK <= (0.5 * EXP_SPAN) ** 2

    kt = k_ref[0, pl.ds(pl.multiple_of(qi * tq, tq), tq), :].astype(F32).T
    own = qf * kt
    m_own = jnp.concatenate([jnp.sum(own[:DHA], axis=0, keepdims=True),
                             jnp.sum(own[DHA:], axis=0, keepdims=True)], axis=1)
    zero_l = jnp.zeros((1, 2 * tq), F32)
    _, l = lax.cond(safe,
                    lambda: sweep(fixed_ref_step)((m_own, zero_l)),
                    lambda: sweep(running_max_step)((jnp.full((1, 2 * tq), NEG_BIG, F32), zero_l)))

    lam = _lambda(lp_ref, lam_init)
    o = acc_ref[0] / l[:, :tq] - lam * (acc_ref[1] / l[:, tq:])
    ms = jnp.mean(o * o, axis=0, keepdims=True)
    o = o * lax.rsqrt(ms + LN_EPS) * g_ref[...] * (1.0 - lam_init)
    o_ref[0] = o.T


def _attn_prompt(lp, g_col, qt, k_bf, vt, tq, chunk, lam_init):
    B, H, nt, _, tk = vt.shape
    S = k_bf.shape[1]
    r = tq // tk
    return pl.pallas_call(
        functools.partial(_attn_prompt_kernel, tq=tq, tk=tk, chunk=chunk, lam_init=lam_init),
        grid=(B, H, S // tq),
        in_specs=[_resident(lp.shape), _resident(g_col.shape),
                  pl.BlockSpec((1, 1, r, DVA, tk), lambda b, h, i: (b, h, i, 0, 0)),
                  pl.BlockSpec((1, S, DVA), lambda b, h, i: (b, 0, h)),
                  pl.BlockSpec((1, 1, nt, DVA, tk), lambda b, h, i: (b, h, 0, 0, 0))],
        out_specs=pl.BlockSpec((1, tq, DVA), lambda b, h, i: (b, i, h)),
        out_shape=jax.ShapeDtypeStruct((B, S, H * DVA), F32),
        scratch_shapes=[pltpu.VMEM((2, DVA, tq), F32), pltpu.VMEM((1, LANES), F32)],
        compiler_params=_cparams(("parallel", "parallel", "arbitrary")),
        name="diff_attn_prompt",
    )(lp, g_col, qt, k_bf, vt)


def _attn_sample_kernel(lp_ref, g_ref, q_ref, ck_ref, cv_ref, nk_ref, nv_ref, o_ref, qq_s, bias_s, m_s, l_s, acc_s,
                        *, lam_init):
    j = pl.program_id(1)
    L = q_ref.shape[1]
    nt = (((1,), (1,)), ((), ()))

    @pl.when(j == 0)
    def _():
        lane = lax.broadcasted_iota(jnp.int32, (L, DVA), 1)
        for mp in range(2):
            keep = (lane < DHA) if mp == 0 else (lane >= DHA)
            for h in range(HA):
                qh = q_ref[0, :, h * DVA:(h + 1) * DVA]
                qq_s[(mp * HA + h) * L:(mp * HA + h + 1) * L, :] = jnp.where(keep, qh, jnp.zeros_like(qh))
        row_head = (lax.broadcasted_iota(jnp.int32, bias_s.shape, 0) // L) % HA
        key_head = lax.broadcasted_iota(jnp.int32, bias_s.shape, 1) % HA
        bias_s[...] = jnp.where(row_head == key_head, 0.0, NEG_BIG)
        m_s[...] = jnp.full(m_s.shape, NEG_BIG, F32)
        l_s[...] = jnp.zeros_like(l_s)
        acc_s[...] = jnp.zeros_like(acc_s)

    def update(kf, vf):
        nk = kf.shape[0]
        s = lax.dot_general(qq_s[...], kf.astype(BF16), nt, preferred_element_type=F32) + bias_s[:, :nk]
        m = m_s[...]
        n = jnp.maximum(m, jnp.max(s, axis=1, keepdims=True))
        p = jnp.exp2(s - n)
        a = jnp.exp2(m - n)
        l_s[...] = a * l_s[...] + jnp.sum(p, axis=1, keepdims=True)
        acc_s[...] = a * acc_s[...] + jnp.dot(p.astype(BF16), vf.astype(BF16), preferred_element_type=F32)
        m_s[...] = n

    update(ck_ref[0, 0], cv_ref[0, 0])

    @pl.when(j == pl.num_programs(1) - 1)
    def _():
        update(nk_ref[0], nv_ref[0])
        lam = _lambda(lp_ref, lam_init)
        o = acc_s[...] / l_s[...]
        for h in range(HA):
            oh = o[h * L:(h + 1) * L] - lam * o[(HA + h) * L:(HA + h + 1) * L]
            ms = jnp.mean(oh * oh, axis=1, keepdims=True)
            o_ref[0, :, h * DVA:(h + 1) * DVA] = oh * lax.rsqrt(ms + LN_EPS) * g_ref[...] * (1.0 - lam_init)


def _attn_sample(lp, g_row, q_bf, ck_all, cv_all, nk, nv, lam_init, layer, tkv):
    B, L, _ = q_bf.shape
    rows = ck_all.shape[2]
    kb = tkv * HA
    per_b = lambda b, j: (b, 0, 0)
    blk = pl.BlockSpec((1, 1, kb, DVA), lambda b, j: (layer, b, j, 0))
    new = pl.BlockSpec((1, L * HA, DVA), per_b)
    nq = 2 * HA * L
    return pl.pallas_call(
        functools.partial(_attn_sample_kernel, lam_init=lam_init),
        grid=(B, rows // kb),
        in_specs=[_resident(lp.shape), _resident(g_row.shape), pl.BlockSpec((1, L, HA * DVA), per_b),
                  blk, blk, new, new],
        out_specs=pl.BlockSpec((1, L, HA * DVA), per_b),
        out_shape=jax.ShapeDtypeStruct((B, L, HA * DVA), F32),
        scratch_shapes=[pltpu.VMEM((nq, DVA), BF16), pltpu.VMEM((nq, kb), F32),
                        pltpu.VMEM((nq, 1), F32), pltpu.VMEM((nq, 1), F32), pltpu.VMEM((nq, DVA), F32)],
        compiler_params=_cparams(("parallel", "arbitrary")),
        name="diff_attn_sample",
    )(lp, g_row, q_bf, ck_all, cv_all, nk, nv)


def _mlstm_kernel(bif_ref, q_ref, k_ref, v_ref, om_ref, gif_ref, wq_ref, wk_ref, bq_ref, bk_ref,
                  cq_ref, ck_ref, c0_ref, n0_ref, m0_ref, mg_ref,
                  ga_ref, oa_ref, gb_ref, x_ref, wo_ref, lng_ref, lnb_ref,
                  xf_ref, xb_ref, c_ref, n_ref, m_ref, qpad, kpad, hm_s, *, lc, nb):
    c = pl.program_id(1)
    npre = CONV_M - 1
    off = SUBLANES - npre

    @pl.when(c == 0)
    def _():
        c_ref[...] = c0_ref[...]
        n_ref[...] = n0_ref[...]
        m_ref[...] = m0_ref[...]
        for r in range(nb):
            qpad[r, 0:SUBLANES, :] = jnp.zeros((SUBLANES, HM * DKM), F32)
            kpad[r, 0:SUBLANES, :] = jnp.zeros((SUBLANES, HM * DKM), F32)
            qpad[r, off:SUBLANES, :] = cq_ref[r]
            kpad[r, off:SUBLANES, :] = ck_ref[r]

    t_i = lax.broadcasted_iota(jnp.int32, (lc, lc), 0)
    s_i = lax.broadcasted_iota(jnp.int32, (lc, lc), 1)
    causal = s_i <= t_i
    zeros = jnp.zeros((lc, lc), F32)
    rows_t = -(-lc // LANES) * LANES

    for r in range(nb):
        qpad[r, SUBLANES:SUBLANES + lc, :] = q_ref[r]
        kpad[r, SUBLANES:SUBLANES + lc, :] = k_ref[r]

        def conv_silu(pad, w_ref, b_ref, cs):
            xa = pad[r, :, cs]
            y = b_ref[:, cs]
            for j in range(CONV_M):
                xs = pltpu.roll(xa, lc + SUBLANES - (off + j), 0)
                y = y + w_ref[j:j + 1, cs] * xs[0:lc, :]
            return y * _sigmoid(y)

        gpre = gif_ref[r]
        if rows_t != lc:
            gpre = jnp.concatenate([gpre, jnp.zeros((rows_t - lc, LANES), F32)], axis=0)
        grow = gpre.T[0:W_GIF, :] + bif_ref[...]
        is_forget = lax.broadcasted_iota(jnp.int32, grow.shape, 0) >= HM
        grow = jnp.where(is_forget, _log_sigmoid(grow), grow)
        gcol = jnp.concatenate([grow, jnp.zeros((LANES - W_GIF, rows_t), F32)], axis=0).T[0:lc, :]
        grow = grow[:, 0:lc]

        for h in range(HM):
            cs = slice(h * DKM, (h + 1) * DKM)
            q = conv_silu(qpad, wq_ref, bq_ref, cs)
            k = conv_silu(kpad, wk_ref, bk_ref, cs) * (DKM ** -0.5)

            ig_c = gcol[:, h:h + 1]
            lf_c = gcol[:, HM + h:HM + h + 1]
            ig_r = grow[h:h + 1, :]
            lf_r = grow[HM + h:HM + h + 1, :]
            bcum_c = jnp.sum(jnp.where(causal, lf_r, zeros), axis=1, keepdims=True)
            bcum_r = jnp.sum(jnp.where(t_i <= s_i, lf_c, zeros), axis=0, keepdims=True)

            m_prev = m_ref[r, h][:, 0:1]
            d = jnp.where(causal, bcum_c - bcum_r + ig_r, NEG_BIG)
            inter = bcum_c + m_prev
            m_t = jnp.maximum(inter, jnp.max(d, axis=1, keepdims=True))
            w = jnp.exp(d - m_t)
            g = jnp.exp(inter - m_t)

            qb = q.astype(BF16)
            kb = k.astype(BF16)
            vb = v_ref[r, :, cs]
            cmat = c_ref[r, h]
            nrow = n_ref[r, h]
            s = lax.dot_general(qb, kb, (((1,), (1,)), ((), ())), preferred_element_type=F32) * w
            num = (jnp.dot(s.astype(BF16), vb, preferred_element_type=F32)
                   + g * jnp.dot(qb, cmat.astype(BF16), preferred_element_type=F32))
            den = jnp.sum(s, axis=1, keepdims=True) + g * jnp.sum(q * nrow, axis=1, keepdims=True)
            den = jnp.maximum(jnp.abs(den), jnp.exp(-m_t))
            hh = num / den

            m_end = m_t[lc - 1:lc, :]
            b_last = bcum_c[lc - 1:lc, :]
            w_end = jnp.exp(b_last - bcum_c + ig_c - m_end)
            g_end = jnp.exp(b_last + m_prev - m_end)
            kw = k * w_end
            c_ref[r, h] = g_end * cmat + lax.dot_general(kw.astype(BF16), vb, (((0,), (0,)), ((), ())),
                                                         preferred_element_type=F32)
            n_ref[r, h] = g_end * nrow + jnp.sum(kw, axis=0, keepdims=True)
            m_ref[r, h] = jnp.broadcast_to(m_end, (1, LANES))

            mu = jnp.mean(hh, axis=1, keepdims=True)
            hc = hh - mu
            var = jnp.mean(hc * hc, axis=1, keepdims=True)
            hm_s[r, :, cs] = hc * lax.rsqrt(var + LN_EPS) * mg_ref[:, cs] * _sigmoid(om_ref[r, :, cs])

        qpad[r, 0:SUBLANES, :] = qpad[r, lc:lc + SUBLANES, :]
        kpad[r, 0:SUBLANES, :] = kpad[r, lc:lc + SUBLANES, :]

        y = _sigmoid(ga_ref[r]) * oa_ref[r] + _sigmoid(gb_ref[r]) * hm_s[r]
        z = jnp.dot(y.astype(BF16), wo_ref[...], preferred_element_type=F32)
        out = _layernorm_rows(ALPHA * x_ref[r] + z, lng_ref[...], lnb_ref[...])
        xf_ref[r] = out
        xb_ref[r] = out.astype(BF16)


def _mlstm_merge(b_if, qkm, vm_bf, om, gif, conv_w, conv_b, conv_state, c0, n0, m0, mh_g,
                 ga, oa, gb, x, w_out, ln_g, ln_b, lc, nb):
    B, S, D = x.shape
    nc = S // lc
    npre = CONV_M - 1
    wq, wv = HM * DKM, HM * DVM
    rowq = lambda b, c: (b, c, 0)
    rowk = lambda b, c: (b, c, 1)
    st = lambda b, c: (b, 0, 0, 0)
    blk = pl.BlockSpec((nb, lc, D), rowq)
    return pl.pallas_call(
        functools.partial(_mlstm_kernel, lc=lc, nb=nb),
        grid=(B // nb, nc),
        in_specs=[_resident(b_if.shape),
                  pl.BlockSpec((nb, lc, wq), rowq), pl.BlockSpec((nb, lc, wq), rowk),
                  pl.BlockSpec((nb, lc, wv), rowq), pl.BlockSpec((nb, lc, wv), rowq),
                  pl.BlockSpec((nb, lc, LANES), rowq),
                  pl.BlockSpec((CONV_M, wq), lambda b, c: (0, 0)), pl.BlockSpec((CONV_M, wq), lambda b, c: (0, 1)),
                  pl.BlockSpec((1, wq), lambda b, c: (0, 0)), pl.BlockSpec((1, wq), lambda b, c: (0, 1)),
                  pl.BlockSpec((nb, npre, wq), lambda b, c: (b, 0, 0)),
                  pl.BlockSpec((nb, npre, wq), lambda b, c: (b, 0, 1)),
                  pl.BlockSpec((nb, HM, DKM, DVM), st), pl.BlockSpec((nb, HM, 1, DKM), st),
                  pl.BlockSpec((nb, HM, 1, LANES), st), _resident(mh_g.shape),
                  blk, blk, blk, blk, _wspec(w_out), _resident(ln_g.shape), _resident(ln_b.shape)],
        out_specs=[blk, blk,
                   pl.BlockSpec((nb, HM, DKM, DVM), st), pl.BlockSpec((nb, HM, 1, DKM), st),
                   pl.BlockSpec((nb, HM, 1, LANES), st)],
        out_shape=[jax.ShapeDtypeStruct((B, S, D), F32), jax.ShapeDtypeStruct((B, S, D), BF16),
                   jax.ShapeDtypeStruct((B, HM, DKM, DVM), F32),
                   jax.ShapeDtypeStruct((B, HM, 1, DKM), F32),
                   jax.ShapeDtypeStruct((B, HM, 1, LANES), F32)],
        scratch_shapes=[pltpu.VMEM((nb, lc + SUBLANES, wq), F32), pltpu.VMEM((nb, lc + SUBLANES, wq), F32),
                        pltpu.VMEM((nb, lc, wv), F32)],
        compiler_params=_cparams(("parallel", "arbitrary")),
        name="mlstm_merge_ln",
    )(b_if, qkm, qkm, vm_bf, om, gif, conv_w, conv_w, conv_b, conv_b, conv_state, conv_state,
      c0, n0, m0, mh_g, ga, oa, gb, x, w_out[0], ln_g, ln_b)


def _ffn_kernel(xb_ref, xf_ref, wu_ref, cw_ref, cb_ref, buf_ref, wd_ref, g_ref, b_ref,
                of_ref, ob_ref, tail_ref, upad, *, seg):
    nseg = xb_ref.shape[0]
    npre = CONV_F - 1
    off = SUBLANES - npre
    stride = seg + SUBLANES

    @pl.when(pl.program_id(1) == 0)
    def _():
        for r in range(nseg):
            upad[r * stride:r * stride + SUBLANES, :] = jnp.zeros((SUBLANES, 2 * D_FF), F32)
            upad[r * stride + off:r * stride + SUBLANES, :] = buf_ref[r]

    x = xb_ref[...].reshape(nseg * seg, D_MODEL)
    chunk = 512
    for c0 in range(0, 2 * D_FF, chunk):
        u = jnp.dot(x, wu_ref[:, c0:c0 + chunk], preferred_element_type=F32)
        for r in range(nseg):
            upad[r * stride + SUBLANES:(r + 1) * stride, c0:c0 + chunk] = u[r * seg:(r + 1) * seg]

    def conv(c0, c1):
        rows = []
        for r in range(nseg):
            y = cb_ref[:, c0:c1]
            for j in range(CONV_F):
                y = y + cw_ref[j:j + 1, c0:c1] * upad[r * stride + off + j:r * stride + off + j + seg, c0:c1]
            rows.append(y)
        return rows[0] if nseg == 1 else jnp.concatenate(rows, axis=0)

    z = None
    for c0 in range(0, D_FF, FFN_COLS):
        a = conv(c0, c0 + FFN_COLS)
        gate = conv(D_FF + c0, D_FF + c0 + FFN_COLS)
        act = 0.5 * a * (1.0 + lax.erf(a * (2.0 ** -0.5)))
        zc = jnp.dot((act * gate).astype(BF16), wd_ref[c0:c0 + FFN_COLS, :], preferred_element_type=F32)
        z = zc if z is None else z + zc

    for r in range(nseg):
        tail = upad[(r + 1) * stride - SUBLANES:(r + 1) * stride, :]
        tail_ref[r] = tail
        upad[r * stride:r * stride + SUBLANES, :] = tail

    out = _layernorm_rows(ALPHA * xf_ref[...].reshape(nseg * seg, D_MODEL) + z, g_ref[...], b_ref[...])
    of_ref[...] = out.reshape(nseg, seg, D_MODEL)
    ob_ref[...] = out.astype(BF16).reshape(nseg, seg, D_MODEL)


def _ffn(x_bf, x_f32, w_up, conv_w, conv_b, buf, w_down, g, b, tm, nseg):
    B, S, D = x_f32.shape
    npre = CONV_F - 1
    row = lambda bb, s: (bb, s, 0)
    blk = pl.BlockSpec((nseg, tm, D), row)
    per_b = lambda bb, s: (bb, 0, 0)
    return pl.pallas_call(
        functools.partial(_ffn_kernel, seg=tm),
        grid=(B // nseg, S // tm),
        in_specs=[blk, blk, _wspec(w_up), _resident(conv_w.shape), _resident(conv_b.shape),
                  pl.BlockSpec((nseg, npre, 2 * D_FF), per_b), _wspec(w_down),
                  _resident(g.shape), _resident(b.shape)],
        out_specs=[blk, blk, pl.BlockSpec((nseg, SUBLANES, 2 * D_FF), per_b)],
        out_shape=[jax.ShapeDtypeStruct((B, S, D), F32), jax.ShapeDtypeStruct((B, S, D), BF16),
                   jax.ShapeDtypeStruct((B, SUBLANES, 2 * D_FF), F32)],
        scratch_shapes=[pltpu.VMEM((nseg * (tm + SUBLANES), 2 * D_FF), F32)],
        compiler_params=_cparams(("parallel", "arbitrary")),
        name="convffn_ln",
    )(x_bf, x_f32, w_up[0], conv_w, conv_b, buf, w_down[0], g, b)


def _rope_tables(pos):
    half = DHA // 2
    inv = ROPE_THETA ** (-jnp.arange(half, dtype=F32) * 2.0 / DHA)
    ang = pos.astype(F32)[:, None] * inv[None, :]
    cos = jnp.tile(jnp.cos(ang), (1, LANES // half))
    sin = jnp.tile(jnp.sin(ang), (1, LANES // half))
    first = (jnp.arange(LANES) % DHA) < half
    return cos, jnp.where(first[None, :], -sin, sin)


def _prepare_weights(w_in, w_out, w_up, w_down):
    offs = [0]
    for n in IN_SIZES:
        offs.append(offs[-1] + n)
    w_in = w_in.astype(BF16)
    ga_w = w_in[:, :, offs[7]:offs[8]]
    gb_w = w_in[:, :, offs[8]:offs[9]]
    gif_w = jnp.pad(w_in[:, :, offs[6]:offs[7]], ((0, 0), (0, 0), (0, LANES - W_GIF)))
    w_out, w_up, w_down = w_out.astype(BF16), w_up.astype(BF16), w_down.astype(BF16)
    cb = D_MODEL
    assert all(offs[i] % cb == 0 for i in (3, 4, 5)) and W_QKM == 2 * cb and W_VM == cb and W_OM == cb
    layers = []
    for l in range(w_in.shape[0]):
        blk = lambda j, l=l: (w_in, (None, D_MODEL, cb), (l, 0, j))
        whole = lambda w, l=l: (w, (None,) + w.shape[1:], (l, 0, 0))
        layers.append(dict(
            qkv=(w_in, (None, D_MODEL, offs[3]), (l, 0, 0)),
            mix=[blk(offs[3] // cb), blk(offs[3] // cb + 1), blk(offs[4] // cb), blk(offs[5] // cb),
                 whole(ga_w), whole(gb_w), whole(gif_w)],
            out=whole(w_out), up=whole(w_up), down=whole(w_down)))
    return layers


MIX_PLAN = ((0, 0), (0, D_MODEL), (1, 0), (2, 0), (3, 0), (4, 0), (5, 0))
MIX_WIDTHS = (W_QKM, W_VM, W_OM, W_GA, W_GB, LANES)
MIX_DTYPES = (F32, BF16, F32, F32, F32, F32)


def _mixer_tail(x_f32, qkm, vm_bf, om, ga, gb, gif, oa, conv_m_buf, m_state, ffn_buf, p, cfg):
    B, S, _ = x_f32.shape
    if m_state is None:
        c0 = jnp.zeros((B, HM, DKM, DVM), F32)
        n0 = jnp.zeros((B, HM, 1, DKM), F32)
        m0 = jnp.zeros((B, HM, 1, LANES), F32)
    else:
        c0 = m_state[0].astype(F32)
        n0 = m_state[1].astype(F32).reshape(B, HM, 1, DKM)
        m0 = jnp.broadcast_to(m_state[2].astype(F32)[:, :, None, None], (B, HM, 1, LANES))
    row = lambda a: a.reshape(1, -1)
    x1_f32, x1_bf, c_new, n_new, m_new = _mlstm_merge(
        p['b_if'].astype(F32).reshape(W_GIF, 1), qkm, vm_bf, om, gif, p['conv_m_w'], row(p['conv_m_b']), conv_m_buf,
        c0, n0, m0, row(p['mh_g']), ga, oa, gb, x_f32, p['w']['out'], row(p['ln1_g']), row(p['ln1_b']),
        cfg['lc'], cfg['nb'])
    new_conv_m = jnp.concatenate([conv_m_buf, qkm[:, S - min(S, CONV_M - 1):]], axis=1)[:, -(CONV_M - 1):]

    x2_f32, x2_bf, tail = _ffn(x1_bf, x1_f32, p['w']['up'], p['ffn_conv_w'], row(p['ffn_conv_b']),
                               ffn_buf, p['w']['down'], row(p['ln2_g']), row(p['ln2_b']),
                               cfg['tm_ffn'], cfg['nb'])
    new_ffn = tail[:, SUBLANES - (CONV_F - 1):]
    return (x2_f32, x2_bf), (new_conv_m, c_new, n_new.reshape(B, HM, DKM), m_new[:, :, 0, 0], new_ffn)


def _layer_prompt(x, layer, kv_all, p, lam_init, cfg):
    x_f32, x_mm = x
    B, S, _ = x_f32.shape
    cos, sin = _rope_tables(jnp.arange(S))
    qt, k_all, k_bf, v_all, vt = _qkv_proj_prompt(x_mm, p['w']['qkv'], cos, sin, cfg['tk'], layer, kv_all)
    qkm, vm_bf, om, ga, gb, gif = _proj(x_mm, p['w']['mix'], MIX_PLAN, MIX_WIDTHS, MIX_DTYPES, cfg['tm_proj'])
    oa = _attn_prompt(p['lam'].astype(F32), p['subln_g'].reshape(DVA, 1), qt, k_bf, vt, cfg['tq'], cfg['chunk'],
                      lam_init)
    conv_m0 = jnp.zeros((B, CONV_M - 1, W_QKM), F32)
    ffn0 = jnp.zeros((B, CONV_F - 1, 2 * D_FF), F32)
    x_new, st = _mixer_tail(x_f32, qkm, vm_bf, om, ga, gb, gif, oa, conv_m0, None, ffn0, p, cfg)
    return x_new, (k_all, v_all), st


def _layer_sample(x, layer, caches, conv_m_buf, m_state, ffn_buf, p, lam_init, cfg):
    x_f32, x_mm = x
    B, S, _ = x_f32.shape
    cos, sin = _rope_tables(PAST_LEN + jnp.arange(S))
    cos, sin = jnp.tile(cos, (B, 1)), jnp.tile(sin, (B, 1))
    xin = x_mm.reshape(1, B * S, D_MODEL)
    unflat = lambda a: a.reshape(B, S, a.shape[-1])
    q_bf, k_f32, v_f32 = map(unflat, _qkv_proj_sample(xin, p['w']['qkv'], cos, sin))
    qkm, vm_bf, om, ga, gb, gif = map(unflat, _proj(xin, p['w']['mix'], MIX_PLAN, MIX_WIDTHS, MIX_DTYPES, B * S))
    oa = _attn_sample(p['lam'].astype(F32), p['subln_g'].reshape(1, DVA), q_bf, caches[0], caches[1],
                      k_f32.reshape(B, S * HA, 2 * DHA), v_f32.reshape(B, S * HA, DVA), lam_init, layer, cfg['tkv'])
    x_new, st = _mixer_tail(x_f32, qkm, vm_bf, om, ga, gb, gif, oa, conv_m_buf, m_state, ffn_buf, p, cfg)
    return x_new, (k_f32.reshape(B, S, HA, 2 * DHA), v_f32.reshape(B, S, HA, DVA)) + st


def _prompt_cfg(S):
    return dict(tm_proj=min(S, 512), tq=min(S, 2048), tk=min(S, 512), chunk=64, lc=min(S, 256), nb=1,
                tm_ffn=min(S, 512))


def _sample_cfg(S, P, B):
    return dict(lc=S, nb=B, tm_ffn=S, tkv=min(P, 512))


def kernel(x_prompt, x_sample, cache_k, cache_v, state_mlstm_conv, state_mlstm_C, state_mlstm_n,
           state_mlstm_m, state_ffn_conv, w_in, b_if, mlstm_conv_w, mlstm_conv_b, diff_lambda,
           diff_subln_g, mlstm_norm_g, w_out, ln1_g, ln1_b, w_up, ffn_conv_w, ffn_conv_b, w_down,
           ln2_g, ln2_b):
    bp, sp, _ = x_prompt.shape
    bs, ss, _ = x_sample.shape
    past = cache_k.shape[2]
    xp = (x_prompt, x_prompt)
    xs = (x_sample, x_sample)
    caches = (cache_k.reshape(DEPTH, bs, past * HA, 2 * DHA), cache_v.reshape(DEPTH, bs, past * HA, DVA))
    weights = _prepare_weights(w_in, w_out, w_up, w_down)
    kv_all = None
    st_p, st_s = [], []
    for l in range(DEPTH):
        p = {'w': weights[l], 'b_if': b_if[l], 'conv_m_w': mlstm_conv_w[l], 'conv_m_b': mlstm_conv_b[l],
             'lam': diff_lambda[l], 'subln_g': diff_subln_g[l], 'mh_g': mlstm_norm_g[l],
             'ln1_g': ln1_g[l], 'ln1_b': ln1_b[l], 'ffn_conv_w': ffn_conv_w[l],
             'ffn_conv_b': ffn_conv_b[l], 'ln2_g': ln2_g[l], 'ln2_b': ln2_b[l]}
        lam_init = 0.8 - 0.6 * math.exp(-0.3 * l)
        xp, kv_all, sp_l = _layer_prompt(xp, l, kv_all, p, lam_init, _prompt_cfg(sp))
        xs, ss_l = _layer_sample(xs, l, caches, state_mlstm_conv[l],
                                 (state_mlstm_C[l], state_mlstm_n[l], state_mlstm_m[l]), state_ffn_conv[l],
                                 p, lam_init, _sample_cfg(ss, past, bs))
        st_p.append(sp_l)
        st_s.append(ss_l)
    stk = lambda lst, i: jnp.stack([s[i] for s in lst])
    k_prompt = kv_all[0].reshape(DEPTH, bp, sp, HA, 2 * DHA)
    v_prompt = kv_all[1].reshape(DEPTH, bp, sp, HA, DVA)
    return ((xp[0], xs[0], k_prompt, v_prompt) + tuple(stk(st_p, i) for i in range(5))
            + tuple(stk(st_s, i) for i in range(7)))
```
